```python
import jax, jax.numpy as jnp
from jax import lax
import numpy as np

D_MODEL = 2048
BATCH = 2
SEQ = 4096
DEPTH = 4

RET_HEADS = 8
RET_QK_DIM = D_MODEL // 16
RET_V_DIM = 2 * RET_QK_DIM
RET_QK_WIDTH = RET_HEADS * RET_QK_DIM
RET_V_WIDTH = RET_HEADS * RET_V_DIM
RET_CHUNK = 128
RET_ROT_BASE = 10000.0
SWA_HEAD_DIM = 64
SWA_Q_HEADS = D_MODEL // SWA_HEAD_DIM
SWA_KV_HEADS = 8
SWA_Q_WIDTH = SWA_Q_HEADS * SWA_HEAD_DIM
SWA_KV_WIDTH = SWA_KV_HEADS * SWA_HEAD_DIM
SWA_WINDOW = 128
SWA_BLOCK = 128
IN_SPLITS = (RET_QK_WIDTH, RET_QK_WIDTH, RET_V_WIDTH, RET_V_WIDTH,
             SWA_Q_WIDTH, SWA_KV_WIDTH, SWA_KV_WIDTH, SWA_Q_WIDTH,
             D_MODEL, D_MODEL)
IN_WIDTH = sum(IN_SPLITS)
EPS = 1e-6

kernel_name = "hybrid_retention_swa_sink_adaln"


def rms_norm(x, w):
    xf = x.astype(jnp.float32)
    y = xf * lax.rsqrt(jnp.mean(xf * xf, axis=-1, keepdims=True) + EPS)
    return (y * w.astype(jnp.float32)).astype(x.dtype)


def rotate(x, pos):
    half = x.shape[-1] // 2
    inv = 1.0 / (RET_ROT_BASE ** jnp.linspace(0.0, 1.0, half, dtype=jnp.float32))
    ang = pos[:, None] * inv[None, :]
    cos = jnp.cos(ang)[None, :, None, :]
    sin = jnp.sin(ang)[None, :, None, :]
    x1, x2 = x[..., :half], x[..., half:]
    return jnp.concatenate([x1 * cos - x2 * sin, x2 * cos + x1 * sin], axis=-1)


def retention(q, k, v):
    b, s, h, _ = q.shape
    nc = s // RET_CHUNK
    log_g = jnp.log1p(-jnp.exp2(-5.0 - jnp.arange(h, dtype=jnp.float32)))
    idx = jnp.arange(RET_CHUNK, dtype=jnp.float32)
    rel = idx[:, None] - idx[None, :]
    decay = jnp.where(rel >= 0, jnp.exp(log_g[:, None, None] * jnp.maximum(rel, 0.0)), 0.0)
    xi = jnp.exp(log_g[None, :] * (idx[:, None] + 1.0))[None, :, :, None]
    zeta = jnp.exp(log_g[None, :] * (RET_CHUNK - 1.0 - idx[:, None]))[None, :, :, None]
    g_chunk = jnp.exp(log_g * RET_CHUNK)[None, :, None, None]

    def to_chunks(t):
        return jnp.moveaxis(t.reshape(b, nc, RET_CHUNK, h, t.shape[-1]), 1, 0)

    def step(state, inp):
        qc, kc, vc = inp
        scores = jnp.einsum('bnhd,bmhd->bhnm', qc, kc) * decay
        inner = jnp.einsum('bhnm,bmhe->bnhe', scores, vc)
        cross = jnp.einsum('bnhd,bhde->bnhe', qc, state) * xi
        new_state = state * g_chunk + jnp.einsum('bmhd,bmhe->bhde', kc * zeta, vc)
        return new_state, inner + cross

    s0 = jnp.zeros((b, h, q.shape[-1], v.shape[-1]), jnp.float32)
    _, out = lax.scan(step, s0, (to_chunks(q), to_chunks(k), to_chunks(v)))
    return jnp.moveaxis(out, 0, 1).reshape(b, s, h, v.shape[-1])


def sliding_window_attention(q, k, v, sinks):
    b, s, hq, dh = q.shape
    hkv = k.shape[2]
    g = hq // hkv
    nb = s // SWA_BLOCK
    qb = q.reshape(b, nb, SWA_BLOCK, hkv, g, dh)

    def band(t):
        tb = t.reshape(b, nb, SWA_BLOCK, hkv, dh)
        prev = jnp.pad(tb, ((0, 0), (1, 0), (0, 0), (0, 0), (0, 0)))[:, :-1]
        return jnp.concatenate([prev, tb], axis=2)

    kb, vb = band(k), band(v)
    scores = jnp.einsum('bnqhgd,bnkhd->bnhgqk', qb, kb).astype(jnp.float32) * (dh ** -0.5)
    qi = jnp.arange(SWA_BLOCK)[:, None]
    ki = jnp.arange(2 * SWA_BLOCK)[None, :]
    dist = qi + SWA_BLOCK - ki
    in_window = (dist >= 0) & (dist < SWA_WINDOW)
    not_first = (jnp.arange(nb) > 0)[:, None, None]
    allowed = in_window[None] & (not_first | (ki >= SWA_BLOCK)[None])
    scores = jnp.where(allowed[None, :, None, None], scores, -jnp.inf)
    sink = sinks.astype(jnp.float32).reshape(hkv, g)[None, None, :, :, None, None]
    m = jnp.maximum(jnp.max(scores, axis=-1, keepdims=True), sink)
    p = jnp.exp(scores - m)
    probs = (p / (jnp.sum(p, axis=-1, keepdims=True) + jnp.exp(sink - m))).astype(v.dtype)
    out = jnp.einsum('bnhgqk,bnkhd->bnqhgd', probs, vb)
    return out.reshape(b, s, hq * dh)


def hybrid_layer(x, c_act, pos, norm_w, ada_w, ada_b, w_in, ret_gn_w, sinks, w_ret_o, w_swa_o, w_out):
    b, s, _ = x.shape
    shift, scale, gate = jnp.split(c_act @ ada_w + ada_b, 3, axis=-1)
    u = rms_norm(x, norm_w) * (1.0 + scale[:, None, :]) + shift[:, None, :]
    proj = u @ w_in
    offsets = []
    acc = 0
    for w in IN_SPLITS[:-1]:
        acc += w
        offsets.append(acc)
    rq, rk, rv, rg, sq, sk, sv, sg, mg_ret, mg_swa = jnp.split(proj, offsets, axis=-1)

    rq = rotate(rq.reshape(b, s, RET_HEADS, RET_QK_DIM).astype(jnp.float32), pos)
    rk = rotate(rk.reshape(b, s, RET_HEADS, RET_QK_DIM).astype(jnp.float32), pos) * (RET_QK_DIM ** -0.5)
    rv = rv.reshape(b, s, RET_HEADS, RET_V_DIM).astype(jnp.float32)
    r = retention(rq, rk, rv)
    mu = jnp.mean(r, axis=-1, keepdims=True)
    var = jnp.mean(jnp.square(r - mu), axis=-1, keepdims=True)
    r = ((r - mu) * lax.rsqrt(var + EPS)).reshape(b, s, RET_V_WIDTH)
    r = (r * ret_gn_w.astype(jnp.float32)).astype(x.dtype) * jax.nn.silu(rg)
    ret_y = r @ w_ret_o

    a = sliding_window_attention(sq.reshape(b, s, SWA_Q_HEADS, SWA_HEAD_DIM),
                                 sk.reshape(b, s, SWA_KV_HEADS, SWA_HEAD_DIM),
                                 sv.reshape(b, s, SWA_KV_HEADS, SWA_HEAD_DIM), sinks)
    swa_y = (a * jax.nn.silu(sg)) @ w_swa_o

    merged = jax.nn.sigmoid(mg_ret) * ret_y + jax.nn.sigmoid(mg_swa) * swa_y
    return x + gate[:, None, :] * (merged @ w_out)


def setup_inputs(seed: int = 0) -> dict:
    key = jax.random.key(seed)
    ks = jax.random.split(key, 14)
    d = D_MODEL
    f32 = jnp.float32
    nrm = lambda k, shape: jax.random.normal(k, shape, f32)
    return {
        "x": nrm(ks[0], (BATCH, SEQ, d)),
        "c": nrm(ks[1], (BATCH, d)),
        "norm_w": 1.0 + 0.02 * nrm(ks[2], (DEPTH, d)),
        "ada_w": nrm(ks[3], (DEPTH, d, 3 * d)) * (0.5 * d ** -0.5),
        "ada_b": 0.01 * nrm(ks[4], (DEPTH, 3 * d)),
        "w_in": nrm(ks[5], (DEPTH, d, IN_WIDTH)) * (d ** -0.5),
        "ret_gn_w": 1.0 + 0.02 * nrm(ks[6], (DEPTH, RET_V_WIDTH)),
        "attn_sinks": nrm(ks[7], (DEPTH, SWA_Q_HEADS)),
        "w_ret_o": nrm(ks[8], (DEPTH, RET_V_WIDTH, d)) * (RET_V_WIDTH ** -0.5),
        "w_swa_o": nrm(ks[9], (DEPTH, SWA_Q_WIDTH, d)) * (SWA_Q_WIDTH ** -0.5),
        "w_out": nrm(ks[10], (DEPTH, d, d)) * (d ** -0.5),
        "final_norm_w": 1.0 + 0.02 * nrm(ks[11], (d,)),
    }


def reference(x, c, norm_w, ada_w, ada_b, w_in, ret_gn_w, attn_sinks, w_ret_o, w_swa_o, w_out, final_norm_w):
    c_act = jax.nn.silu(c)
    pos = jnp.arange(x.shape[1], dtype=jnp.float32)
    h = x
    for l in range(DEPTH):
        h = hybrid_layer(h, c_act, pos, norm_w[l], ada_w[l], ada_b[l], w_in[l], ret_gn_w[l],
                         attn_sinks[l], w_ret_o[l], w_swa_o[l], w_out[l])
    return rms_norm(h, final_norm_w)
```

```python
import functools

import numpy as np
import jax
import jax.numpy as jnp
from jax import lax
from jax.experimental import pallas as pl
from jax.experimental.pallas import tpu as pltpu

F32 = jnp.float32
BF16 = jnp.bfloat16

RET_HEADS = 8
RET_CHUNK = 128
RET_ROT_BASE = 10000.0
SWA_HEAD_DIM = 64
SWA_KV_HEADS = 8
SWA_BLOCK = 128
EPS = 1e-6

V7X_LANES = 128
V7X_VMEM_BYTES = 64 * 1024 * 1024
VMEM_LIMIT_BYTES = V7X_VMEM_BYTES - 8 * 1024 * 1024

NEG_BIG = -1e30


def _params(semantics):
    return pltpu.CompilerParams(dimension_semantics=semantics, vmem_limit_bytes=VMEM_LIMIT_BYTES)


def _sigmoid(y):
    return 1.0 / (1.0 + jnp.exp(-y))


def _mod_kernel(c_ref, w_ref, b_ref, o_ref):
    c = c_ref[...]
    ca = c * _sigmoid(c)
    o_ref[...] = jnp.dot(ca, w_ref[...], preferred_element_type=F32,
                         precision=lax.Precision.HIGHEST) + b_ref[...]


def _modulation(c_pad, ada_w, ada_b3, tn):
    depth, d, n3 = ada_w.shape
    rows = c_pad.shape[0]
    return pl.pallas_call(
        _mod_kernel,
        grid=(depth, n3 // tn),
        in_specs=[
            pl.BlockSpec((rows, d), lambda l, j: (0, 0)),
            pl.BlockSpec((None, d, tn), lambda l, j: (l, 0, j)),
            pl.BlockSpec((None, 1, tn), lambda l, j: (l, 0, j)),
        ],
        out_specs=pl.BlockSpec((None, rows, tn), lambda l, j: (l, 0, j)),
        out_shape=jax.ShapeDtypeStruct((depth, rows, n3), F32),
        compiler_params=_params(("parallel", "parallel")),
        name="adaln_modulation",
    )(c_pad, ada_w, ada_b3)


def _modulated_norm(h, nw, scale, shift):
    ms = jnp.mean(h * h, axis=-1, keepdims=True)
    y = h * lax.rsqrt(ms + EPS) * nw
    return y * (1.0 + scale) + shift


def _u_kernel(x_ref, nw_ref, sc_ref, sh_ref, u_ref):
    u_ref[...] = _modulated_norm(x_ref[...], nw_ref[...], sc_ref[...], sh_ref[...]).astype(BF16)


def _first_u(x2, nw, scale, shift, seq, tm):
    m, d = x2.shape
    per_batch = seq // tm
    return pl.pallas_call(
        _u_kernel,
        grid=(m // tm,),
        in_specs=[
            pl.BlockSpec((tm, d), lambda i: (i, 0)),
            pl.BlockSpec((1, d), lambda i: (0, 0)),
            pl.BlockSpec((None, 1, d), lambda i: (i // per_batch, 0, 0)),
            pl.BlockSpec((None, 1, d), lambda i: (i // per_batch, 0, 0)),
        ],
        out_specs=pl.BlockSpec((tm, d), lambda i: (i, 0)),
        out_shape=jax.ShapeDtypeStruct((m, d), BF16),
        compiler_params=_params(("parallel",)),
        name="first_modulated_norm",
    )(x2, nw, scale, shift)


def _inproj_kernel(*refs, kind):
    if kind == "rot":
        u_ref, w_ref, cos_ref, sin_ref, o_ref, wbf_ref = refs
    else:
        u_ref, w_ref, o_ref, wbf_ref = refs

    @pl.when(pl.program_id(1) == 0)
    def _():
        wbf_ref[...] = w_ref[...].astype(BF16)

    y = jnp.dot(u_ref[...], wbf_ref[...], preferred_element_type=F32)
    if kind == "plain":
        o_ref[...] = y.astype(BF16)
    elif kind == "silu":
        o_ref[...] = (y * _sigmoid(y)).astype(BF16)
    elif kind == "sigmoid":
        o_ref[...] = _sigmoid(y).astype(BF16)
    else:
        cs = cos_ref[...]
        sn = sin_ref[...]
        hd = cs.shape[-1]
        for h in range(y.shape[1] // hd):
            yh = y[:, h * hd:(h + 1) * hd]
            o_ref[:, h * hd:(h + 1) * hd] = (yh * cs + pltpu.roll(yh, hd // 2, 1) * sn).astype(BF16)


def _inproj(u, w_in, layer, col_map, n_tiles, kind, tm, tn, seq, tables=None):
    m, d = u.shape
    in_specs = [
        pl.BlockSpec((tm, d), lambda j, i: (i, 0)),
        pl.BlockSpec((None, d, tn), lambda j, i: (layer, 0, col_map(j))),
    ]
    args = [u, w_in]
    if kind == "rot":
        per_batch = seq // tm
        hd = tables[0].shape[-1]
        tab_spec = pl.BlockSpec((None, tm, hd), lambda j, i: (j, i % per_batch, 0))
        in_specs += [tab_spec, tab_spec]
        args += list(tables)
    return pl.pallas_call(
        functools.partial(_inproj_kernel, kind=kind),
        grid=(n_tiles, m // tm),
        in_specs=in_specs,
        out_specs=pl.BlockSpec((tm, tn), lambda j, i: (i, j)),
        out_shape=jax.ShapeDtypeStruct((m, n_tiles * tn), BF16),
        scratch_shapes=[pltpu.VMEM((d, tn), BF16)],
        compiler_params=_params(("arbitrary", "arbitrary")),
        name="inproj_" + kind,
    )(*args)


def _retention_kernel(qk_ref, v_ref, g_ref, gnw_ref, o_ref, state_ref, *, heads, chunk, log_gammas):
    @pl.when(pl.program_id(1) == 0)
    def _():
        state_ref[...] = jnp.zeros_like(state_ref)

    t = qk_ref.shape[0]
    dk = qk_ref.shape[1] // (2 * heads)
    dv = v_ref.shape[1] // heads
    row = lax.broadcasted_iota(jnp.int32, (chunk, chunk), 0)
    col = lax.broadcasted_iota(jnp.int32, (chunk, chunk), 1)
    rel = (row - col).astype(F32)
    pos = lax.broadcasted_iota(jnp.int32, (chunk, 1), 0).astype(F32)

    for h in range(heads):
        lg = log_gammas[h]
        decay = jnp.where(rel >= 0.0, jnp.exp(lg * jnp.maximum(rel, 0.0)), 0.0)
        xi = jnp.exp(lg * (pos + 1.0))
        zeta = jnp.exp(lg * (chunk - 1.0 - pos))
        g_chunk = float(np.exp(lg * chunk))
        gnw = gnw_ref[:, h * dv:(h + 1) * dv]
        for c in range(t // chunk):
            rows = slice(c * chunk, (c + 1) * chunk)
            q = qk_ref[rows, h * dk:(h + 1) * dk]
            k = qk_ref[rows, (heads + h) * dk:(heads + h + 1) * dk]
            v = v_ref[rows, h * dv:(h + 1) * dv]
            st = state_ref[h]
            s = lax.dot_general(q, k, (((1,), (1,)), ((), ())), preferred_element_type=F32)
            lhs = jnp.concatenate([(s * decay).astype(BF16), (q.astype(F32) * xi).astype(BF16)], axis=1)
            rhs = jnp.concatenate([v, st.astype(BF16)], axis=0)
            o = jnp.dot(lhs, rhs, preferred_element_type=F32)
            kz = (k.astype(F32) * zeta).astype(BF16)
            kv = lax.dot_general(kz, v, (((0,), (0,)), ((), ())), preferred_element_type=F32)
            state_ref[h] = st * g_chunk + kv
            mu = jnp.mean(o, axis=-1, keepdims=True)
            dlt = o - mu
            var = jnp.mean(dlt * dlt, axis=-1, keepdims=True)
            rn = dlt * lax.rsqrt(var + EPS) * gnw
            o_ref[rows, h * dv:(h + 1) * dv] = (rn * g_ref[rows, h * dv:(h + 1) * dv].astype(F32)).astype(BF16)


def _retention(rqk, plain, gates, gn_w3, layer, seq, t):
    m, qk_w = rqk.shape
    v_w = gn_w3.shape[-1]
    per_batch = seq // t
    log_gammas = tuple(float(np.log1p(-np.exp2(-5.0 - h))) for h in range(RET_HEADS))
    dk = qk_w // (2 * RET_HEADS)
    dv = v_w // RET_HEADS
    row_map = lambda b, s: (b * per_batch + s, 0)
    return pl.pallas_call(
        functools.partial(_retention_kernel, heads=RET_HEADS, chunk=RET_CHUNK, log_gammas=log_gammas),
        grid=(m // seq, per_batch),
        in_specs=[
            pl.BlockSpec((t, qk_w), row_map),
            pl.BlockSpec((t, v_w), row_map),
            pl.BlockSpec((t, v_w), row_map),
            pl.BlockSpec((None, 1, v_w), lambda b, s: (layer, 0, 0)),
        ],
        out_specs=pl.BlockSpec((t, v_w), row_map),
        out_shape=jax.ShapeDtypeStruct((m, v_w), BF16),
        scratch_shapes=[pltpu.VMEM((RET_HEADS, dk, dv), F32)],
        compiler_params=_params(("parallel", "arbitrary")),
        name="retention",
    )(rqk, plain, gates, gn_w3)


def _swa_kernel(sink_ref, q_ref, kp_ref, kc_ref, vp_ref, vc_ref, g_ref, o_ref, *, kv_heads, group, hd):
    blk = q_ref.shape[0]
    lanes = 2 * hd
    first = pl.program_id(1) == 0
    lane = lax.broadcasted_iota(jnp.int32, (1, lanes), 1)
    lo_f = lane < hd
    scale = float(hd) ** -0.5
    qmask = (jnp.where(lo_f, scale, 0.0).astype(BF16), jnp.where(lo_f, 0.0, scale).astype(BF16))
    row = lax.broadcasted_iota(jnp.int32, (blk, blk), 0)
    col = lax.broadcasted_iota(jnp.int32, (blk, blk), 1)
    upper = col > row

    def dup(tile_bf16, parity):
        tile = tile_bf16.astype(F32)
        rolled = pltpu.roll(tile, hd, 1)
        keep_lo = lo_f if parity == 0 else jnp.logical_not(lo_f)
        return jnp.where(keep_lo, tile, rolled).astype(BF16)

    for h in range(kv_heads):
        tcol = slice((h // 2) * lanes, (h // 2 + 1) * lanes)
        par = h % 2
        k2 = jnp.concatenate([dup(kp_ref[:, tcol], par), dup(kc_ref[:, tcol], par)], axis=0)
        v2 = jnp.concatenate([dup(vp_ref[:, tcol], par), dup(vc_ref[:, tcol], par)], axis=0)
        for gp in range(group // 2):
            qt = (h * group) // 2 + gp
            qcol = slice(qt * lanes, (qt + 1) * lanes)
            qtile = q_ref[:, qcol]
            outs = []
            for e in range(2):
                sink = sink_ref[h * group + 2 * gp + e]
                s2 = lax.dot_general(qtile * qmask[e], k2, (((1,), (1,)), ((), ())),
                                     preferred_element_type=F32)
                s_prev = jnp.where(first, NEG_BIG, s2[:, :blk])
                s = jnp.where(upper, s_prev, s2[:, blk:])
                mx = jnp.maximum(jnp.max(s, axis=-1, keepdims=True), sink)
                p = jnp.exp(s - mx)
                denom = jnp.sum(p, axis=-1, keepdims=True) + jnp.exp(sink - mx)
                p2 = jnp.concatenate([jnp.where(upper, p, 0.0), jnp.where(upper, 0.0, p)], axis=1)
                o2 = jnp.dot(p2.astype(BF16), v2, preferred_element_type=F32)
                outs.append(o2 * (1.0 / denom))
            pair = jnp.where(lo_f, outs[0], outs[1])
            o_ref[:, qcol] = (pair * g_ref[:, qcol].astype(F32)).astype(BF16)


def _swa(plain, gates, sinks_l, seq, q_w, kv_w, ret_v_w):
    m = plain.shape[0]
    nb = seq // SWA_BLOCK
    kv_heads = SWA_KV_HEADS
    group = q_w // SWA_HEAD_DIM // kv_heads
    q_blk = ret_v_w // q_w
    k_blk = (ret_v_w + q_w) // kv_w
    v_blk = k_blk + 1
    g_blk = ret_v_w // q_w
    cur = lambda b, n: b * nb + n
    prev = lambda b, n: b * nb + jnp.maximum(n - 1, 0)
    return pl.pallas_call(
        functools.partial(_swa_kernel, kv_heads=kv_heads, group=group, hd=SWA_HEAD_DIM),
        grid=(m // seq, nb),
        in_specs=[
            pl.BlockSpec(memory_space=pltpu.SMEM),
            pl.BlockSpec((SWA_BLOCK, q_w), lambda b, n: (cur(b, n), q_blk)),
            pl.BlockSpec((SWA_BLOCK, kv_w), lambda b, n: (prev(b, n), k_blk)),
            pl.BlockSpec((SWA_BLOCK, kv_w), lambda b, n: (cur(b, n), k_blk)),
            pl.BlockSpec((SWA_BLOCK, kv_w), lambda b, n: (prev(b, n), v_blk)),
            pl.BlockSpec((SWA_BLOCK, kv_w), lambda b, n: (cur(b, n), v_blk)),
            pl.BlockSpec((SWA_BLOCK, q_w), lambda b, n: (cur(b, n), g_blk)),
        ],
        out_specs=pl.BlockSpec((SWA_BLOCK, q_w), lambda b, n: (cur(b, n), 0)),
        out_shape=jax.ShapeDtypeStruct((m, q_w), BF16),
        compiler_params=_params(("parallel", "arbitrary")),
        name="swa",
    )(sinks_l, plain, plain, plain, plain, plain, gates)


def _outproj_kernel(*refs, last):
    if last:
        r_ref, a_ref, sg_ref, h_ref, gate_ref, wr_ref, ws_ref, wo_ref, fw_ref, out_ref = refs
    else:
        (r_ref, a_ref, sg_ref, h_ref, gate_ref, wr_ref, ws_ref, wo_ref,
         nw_ref, sc_ref, sh_ref, hn_ref, u_ref) = refs
    d = h_ref.shape[1]
    y1 = jnp.dot(r_ref[...], wr_ref[...], preferred_element_type=F32)
    y2 = jnp.dot(a_ref[...], ws_ref[...], preferred_element_type=F32)
    merged = sg_ref[:, :d].astype(F32) * y1 + sg_ref[:, d:].astype(F32) * y2
    z = jnp.dot(merged.astype(BF16), wo_ref[...], preferred_element_type=F32)
    hn = h_ref[...] + gate_ref[...] * z
    if last:
        ms = jnp.mean(hn * hn, axis=-1, keepdims=True)
        out_ref[...] = hn * lax.rsqrt(ms + EPS) * fw_ref[...]
    else:
        hn_ref[...] = hn
        u_ref[...] = _modulated_norm(hn, nw_ref[...], sc_ref[...], sh_ref[...]).astype(BF16)


def _outproj(r, a, sig, h, gate, wr, ws, wo, layer, seq, tm, nxt=None, final_w=None):
    m, d = h.shape
    per_batch = seq // tm
    last = nxt is None
    row = lambda i: (i, 0)
    vec = pl.BlockSpec((None, 1, d), lambda i: (i // per_batch, 0, 0))
    wspec = pl.BlockSpec((None, d, d), lambda i: (layer, 0, 0), pipeline_mode=pl.Buffered(1))
    in_specs = [
        pl.BlockSpec((tm, d), row), pl.BlockSpec((tm, d), row), pl.BlockSpec((tm, 2 * d), row),
        pl.BlockSpec((tm, d), row), vec, wspec, wspec, wspec,
    ]
    args = [r, a, sig, h, gate, wr, ws, wo]
    if last:
        in_specs.append(pl.BlockSpec((1, d), lambda i: (0, 0)))
        args.append(final_w)
        out_specs = pl.BlockSpec((tm, d), row)
        out_shape = jax.ShapeDtypeStruct((m, d), F32)
    else:
        nw, sc, sh = nxt
        in_specs += [pl.BlockSpec((1, d), lambda i: (0, 0)), vec, vec]
        args += [nw, sc, sh]
        out_specs = (pl.BlockSpec((tm, d), row), pl.BlockSpec((tm, d), row))
        out_shape = (jax.ShapeDtypeStruct((m, d), F32), jax.ShapeDtypeStruct((m, d), BF16))
    return pl.pallas_call(
        functools.partial(_outproj_kernel, last=last),
        grid=(m // tm,),
        in_specs=in_specs,
        out_specs=out_specs,
        out_shape=out_shape,
        compiler_params=_params(("parallel",)),
        name="outproj_last" if last else "outproj",
    )(*args)


def kernel(x, c, norm_w, ada_w, ada_b, w_in, ret_gn_w, attn_sinks, w_ret_o, w_swa_o, w_out, final_norm_w):
    b, s, d = x.shape
    depth = norm_w.shape[0]
    m = b * s
    ret_v_w = ret_gn_w.shape[1]
    ret_qk_w = ret_v_w // 2
    swa_q_w = d
    swa_kv_w = SWA_KV_HEADS * SWA_HEAD_DIM
    tn = 1024
    tm = 1024
    assert w_in.shape[2] == 2 * ret_qk_w + 2 * ret_v_w + 2 * swa_q_w + 2 * swa_kv_w + 2 * d
    assert ret_qk_w == tn and 2 * swa_kv_w == tn and s % tm == 0

    c_pad = jnp.pad(c, ((0, 8 - b), (0, 0)))
    mod = _modulation(c_pad, ada_w, ada_b.reshape(depth, 1, 3 * d), tn=1536)[:, :b]
    shift = mod[:, :, :d].reshape(depth, b, 1, d)
    scale = mod[:, :, d:2 * d].reshape(depth, b, 1, d)
    gate = mod[:, :, 2 * d:].reshape(depth, b, 1, d)

    half = ret_qk_w // RET_HEADS // 2
    inv = 1.0 / (RET_ROT_BASE ** jnp.linspace(0.0, 1.0, half, dtype=F32))
    ang = jnp.arange(s, dtype=F32)[:, None] * inv[None, :]
    cos2 = jnp.concatenate([jnp.cos(ang), jnp.cos(ang)], axis=-1)
    sin2 = jnp.concatenate([-jnp.sin(ang), jnp.sin(ang)], axis=-1)
    k_scale = float(2 * half) ** -0.5
    cos_tab = jnp.stack([cos2, cos2 * k_scale])
    sin_tab = jnp.stack([sin2, sin2 * k_scale])

    wr_bf = w_ret_o.astype(BF16)
    ws_bf = w_swa_o.astype(BF16)
    wo_bf = w_out.astype(BF16)
    gn_w3 = ret_gn_w.reshape(depth, 1, ret_v_w)
    nw2 = norm_w.reshape(depth, 1, d)

    h = x.reshape(m, d)
    u = _first_u(h, nw2[0], scale[0], shift[0], s, tm=512)

    for l in range(depth):
        rqk = _inproj(u, w_in, l, lambda j: j, 2, "rot", tm, tn, s, tables=(cos_tab, sin_tab))
        plain = _inproj(u, w_in, l, lambda j: jnp.where(j < 2, j + 2, j + 4), 5, "plain", tm, tn, s)
        gates = _inproj(u, w_in, l, lambda j: jnp.where(j < 2, j + 4, j + 7), 4, "silu", tm, tn, s)
        sig = _inproj(u, w_in, l, lambda j: j + 11, 4, "sigmoid", tm, tn, s)
        r = _retention(rqk, plain, gates, gn_w3, l, s, t=512)
        a = _swa(plain, gates, attn_sinks[l], s, swa_q_w, swa_kv_w, ret_v_w)
        if l + 1 < depth:
            h, u = _outproj(r, a, sig, h, gate[l], wr_bf, ws_bf, wo_bf, l, s, tm=256,
                            nxt=(nw2[l + 1], scale[l + 1], shift[l + 1]))
        else:
            h = _outproj(r, a, sig, h, gate[l], wr_bf, ws_bf, wo_bf, l, s, tm=256,
                         final_w=final_norm_w.reshape(1, d))
    return h.reshape(b, s, d)
```

```python
import functools

import numpy as np
import jax
import jax.numpy as jnp
from jax import lax
from jax.experimental import pallas as pl
from jax.experimental.pallas import tpu as pltpu

F32 = jnp.float32
BF16 = jnp.bfloat16

RET_HEADS = 8
RET_CHUNK = 128
RET_ROT_BASE = 10000.0
SWA_HEAD_DIM = 64
SWA_KV_HEADS = 8
SWA_BLOCK = 128
EPS = 1e-6

V7X_LANES = 128
V7X_VMEM_BYTES = 64 * 1024 * 1024
VMEM_LIMIT_BYTES = V7X_VMEM_BYTES - 8 * 1024 * 1024

NEG_BIG = -1e30


def _params(semantics):
    return pltpu.CompilerParams(dimension_semantics=semantics, vmem_limit_bytes=VMEM_LIMIT_BYTES)


def _sigmoid(y):
    return 1.0 / (1.0 + jnp.exp(-y))


def _mod_kernel(c_ref, w_ref, b_ref, o_ref):
    c = c_ref[...]
    ca = c * _sigmoid(c)
    o_ref[...] = jnp.dot(ca, w_ref[...], preferred_element_type=F32,
                         precision=lax.Precision.HIGHEST) + b_ref[...]


def _modulation(c_pad, ada_w, ada_b3, tn):
    depth, d, n3 = ada_w.shape
    rows = c_pad.shape[0]
    return pl.pallas_call(
        _mod_kernel,
        grid=(depth, n3 // tn),
        in_specs=[
            pl.BlockSpec((rows, d), lambda l, j: (0, 0)),
            pl.BlockSpec((None, d, tn), lambda l, j: (l, 0, j)),
            pl.BlockSpec((None, 1, tn), lambda l, j: (l, 0, j)),
        ],
        out_specs=pl.BlockSpec((None, rows, tn), lambda l, j: (l, 0, j)),
        out_shape=jax.ShapeDtypeStruct((depth, rows, n3), F32),
        compiler_params=_params(("parallel", "parallel")),
        name="adaln_modulation",
    )(c_pad, ada_w, ada_b3)


def _modulated_norm(h, nw, scale, shift):
    ms = jnp.mean(h * h, axis=-1, keepdims=True)
    y = h * lax.rsqrt(ms + EPS) * nw
    return y * (1.0 + scale) + shift


def _u_kernel(x_ref, nw_ref, sc_ref, sh_ref, u_ref):
    u_ref[...] = _modulated_norm(x_ref[...], nw_ref[...], sc_ref[...], sh_ref[...]).astype(BF16)


def _first_u(x2, nw, scale, shift, seq, tm):
    m, d = x2.shape
    per_batch = seq // tm
    return pl.pallas_call(
        _u_kernel,
        grid=(m // tm,),
        in_specs=[
            pl.BlockSpec((tm, d), lambda i: (i, 0)),
            pl.BlockSpec((1, d), lambda i: (0, 0)),
            pl.BlockSpec((None, 1, d), lambda i: (i // per_batch, 0, 0)),
            pl.BlockSpec((None, 1, d), lambda i: (i // per_batch, 0, 0)),
        ],
        out_specs=pl.BlockSpec((tm, d), lambda i: (i, 0)),
        out_shape=jax.ShapeDtypeStruct((m, d), BF16),
        compiler_params=_params(("parallel",)),
        name="first_modulated_norm",
    )(x2, nw, scale, shift)


def _inproj_kernel(*refs, kind):
    if kind == "rot":
        u_ref, w_ref, cos_ref, sin_ref, o_ref, wbf_ref = refs
    else:
        u_ref, w_ref, o_ref, wbf_ref = refs

    @pl.when(pl.program_id(1) == 0)
    def _():
        wbf_ref[...] = w_ref[...].astype(BF16)

    y = jnp.dot(u_ref[...], wbf_ref[...], preferred_element_type=F32)
    if kind == "plain":
        o_ref[...] = y.astype(BF16)
    elif kind == "silu":
        o_ref[...] = (y * _sigmoid(y)).astype(BF16)
    elif kind == "sigmoid":
        o_ref[...] = _sigmoid(y).astype(BF16)
    else:
        cs = cos_ref[...]
        sn = sin_ref[...]
        hd = cs.shape[-1]
        for h in range(y.shape[1] // hd):
            yh = y[:, h * hd:(h + 1) * hd]
            o_ref[:, h * hd:(h + 1) * hd] = (yh * cs + pltpu.roll(yh, hd // 2, 1) * sn).astype(BF16)


def _inproj(u, w_in, layer, col_map, n_tiles, kind, tm, tn, seq, tables=None):
    m, d = u.shape
    in_specs = [
        pl.BlockSpec((tm, d), lambda j, i: (i, 0)),
        pl.BlockSpec((None, d, tn), lambda j, i: (layer, 0, col_map(j))),
    ]
    args = [u, w_in]
    if kind == "rot":
        per_batch = seq // tm
        hd = tables[0].shape[-1]
        tab_spec = pl.BlockSpec((None, tm, hd), lambda j, i: (j, i % per_batch, 0))
        in_specs += [tab_spec, tab_spec]
        args += list(tables)
    return pl.pallas_call(
        functools.partial(_inproj_kernel, kind=kind),
        grid=(n_tiles, m // tm),
        in_specs=in_specs,
        out_specs=pl.BlockSpec((tm, tn), lambda j, i: (i, j)),
        out_shape=jax.ShapeDtypeStruct((m, n_tiles * tn), BF16),
        scratch_shapes=[pltpu.VMEM((d, tn), BF16)],
        compiler_params=_params(("arbitrary", "arbitrary")),
        name="inproj_" + kind,
    )(*args)


def _retention_kernel(qk_ref, v_ref, g_ref, gnw_ref, o_ref, state_ref, *, heads, chunk, log_gammas):
    @pl.when(pl.program_id(1) == 0)
    def _():
        state_ref[...] = jnp.zeros_like(state_ref)

    t = qk_ref.shape[0]
    dk = qk_ref.shape[1] // (2 * heads)
    dv = v_ref.shape[1] // heads
    row = lax.broadcasted_iota(jnp.int32, (chunk, chunk), 0)
    col = lax.broadcasted_iota(jnp.int32, (chunk, chunk), 1)
    rel = (row - col).astype(F32)
    pos = lax.broadcasted_iota(jnp.int32, (chunk, 1), 0).astype(F32)

    for h in range(heads):
        lg = log_gammas[h]
        decay = jnp.where(rel >= 0.0, jnp.exp(lg * jnp.maximum(rel, 0.0)), 0.0)
        xi = jnp.exp(lg * (pos + 1.0))
        zeta = jnp.exp(lg * (chunk - 1.0 - pos))
        g_chunk = float(np.exp(lg * chunk))
        gnw = gnw_ref[:, h * dv:(h + 1) * dv]
        for c in range(t // chunk):
            rows = slice(c * chunk, (c + 1) * chunk)
            q = qk_ref[rows, h * dk:(h + 1) * dk]
            k = qk_ref[rows, (heads + h) * dk:(heads + h + 1) * dk]
            v = v_ref[rows, h * dv:(h + 1) * dv]
            st = state_ref[h]
            s = lax.dot_general(q, k, (((1,), (1,)), ((), ())), preferred_element_type=F32)
            lhs = jnp.concatenate([(s * decay).astype(BF16), (q.astype(F32) * xi).astype(BF16)], axis=1)
            rhs = jnp.concatenate([v, st.astype(BF16)], axis=0)
            o = jnp.dot(lhs, rhs, preferred_element_type=F32)
            kz = (k.astype(F32) * zeta).astype(BF16)
            kv = lax.dot_general(kz, v, (((0,), (0,)), ((), ())), preferred_element_type=F32)
            state_ref[h] = st * g_chunk + kv
            mu = jnp.mean(o, axis=-1, keepdims=True)
            dlt = o - mu
            var = jnp.mean(dlt * dlt, axis=-1, keepdims=True)
            rn = dlt * lax.rsqrt(var + EPS) * gnw
            o_ref[rows, h * dv:(h + 1) * dv] = (rn * g_ref[rows, h * dv:(h + 1) * dv].astype(F32)).astype(BF16)


def _retention(rqk, plain, gates, gn_w3, layer, seq, t):
    m, qk_w = rqk.shape
    v_w = gn_w3.shape[-1]
    per_batch = seq // t
    log_gammas = tuple(float(np.log1p(-np.exp2(-5.0 - h))) for h in range(RET_HEADS))
    dk = qk_w // (2 * RET_HEADS)
    dv = v_w // RET_HEADS
    row_map = lambda b, s: (b * per_batch + s, 0)
    return pl.pallas_call(
        functools.partial(_retention_kernel, heads=RET_HEADS, chunk=RET_CHUNK, log_gammas=log_gammas),
        grid=(m // seq, per_batch),
        in_specs=[
            pl.BlockSpec((t, qk_w), row_map),
            pl.BlockSpec((t, v_w), row_map),
            pl.BlockSpec((t, v_w), row_map),
            pl.BlockSpec((None, 1, v_w), lambda b, s: (layer, 0, 0)),
        ],
        out_specs=pl.BlockSpec((t, v_w), row_map),
        out_shape=jax.ShapeDtypeStruct((m, v_w), BF16),
        scratch_shapes=[pltpu.VMEM((RET_HEADS, dk, dv), F32)],
        compiler_params=_params(("parallel", "arbitrary")),
        name="retention",
    )(rqk, plain, gates, gn_w3)


def _swa_body(sink_ref, q_ref, kp_ref, kc_ref, vp_ref, vc_ref, g_ref, o_ref, *, kv_heads, group, hd, first):
    blk = q_ref.shape[0]
    lanes = 2 * hd
    log2e = float(np.log2(np.e))
    lane = lax.broadcasted_iota(jnp.int32, (1, lanes), 1)
    lo_f = lane < hd
    scale = float(hd) ** -0.5 * log2e
    qmask = (jnp.where(lo_f, scale, 0.0).astype(BF16), jnp.where(lo_f, 0.0, scale).astype(BF16))
    key = lax.broadcasted_iota(jnp.int32, (blk, 2 * blk), 0)
    qry = jnp.bitwise_and(lax.broadcasted_iota(jnp.int32, (blk, 2 * blk), 1), blk - 1)
    upper = key > qry
    head_lo = lax.broadcasted_iota(jnp.int32, (1, 2 * blk), 1) < blk

    def dup(tile_bf16, parity):
        tile = tile_bf16.astype(F32)
        rolled = pltpu.roll(tile, hd, 1)
        keep_lo = lo_f if parity == 0 else jnp.logical_not(lo_f)
        return jnp.where(keep_lo, tile, rolled)

    for h in range(kv_heads):
        tcol = slice((h // 2) * lanes, (h // 2 + 1) * lanes)
        par = h % 2
        hrows = slice(par * hd, (par + 1) * hd)
        k_cur = dup(kc_ref[:, tcol], par).astype(BF16)
        if par == 0:
            vt_tile_cur = vc_ref[:, tcol].astype(F32).T.astype(BF16)
        vt_cur = vt_tile_cur
        if not first:
            k_prev = dup(kp_ref[:, tcol], par).astype(BF16)
            if par == 0:
                vt_tile_prev = vp_ref[:, tcol].astype(F32).T.astype(BF16)
            vt_prev = vt_tile_prev
        for gp in range(group // 2):
            qt = (h * group) // 2 + gp
            qcol = slice(qt * lanes, (qt + 1) * lanes)
            qtile = q_ref[:, qcol]
            qq = jnp.concatenate([qtile * qmask[0], qtile * qmask[1]], axis=0)
            nt = (((1,), (1,)), ((), ()))
            s_cur = lax.dot_general(k_cur, qq, nt, preferred_element_type=F32)
            if first:
                s = jnp.where(upper, NEG_BIG, s_cur)
            else:
                s = jnp.where(upper, lax.dot_general(k_prev, qq, nt, preferred_element_type=F32), s_cur)
            hq = h * group + 2 * gp
            sink = jnp.where(head_lo, sink_ref[hq], sink_ref[hq + 1]) * log2e
            mx = jnp.maximum(jnp.max(s, axis=0, keepdims=True), sink)
            p = jnp.exp2(s - mx)
            inv = 1.0 / (jnp.sum(p, axis=0, keepdims=True) + jnp.exp2(sink - mx))
            if first:
                ot = jnp.dot(vt_cur, p.astype(BF16), preferred_element_type=F32)
            else:
                ot = (jnp.dot(vt_prev, jnp.where(upper, p, 0.0).astype(BF16), preferred_element_type=F32)
                      + jnp.dot(vt_cur, jnp.where(upper, 0.0, p).astype(BF16), preferred_element_type=F32))
            ot = ot[hrows] * inv
            pair_t = jnp.concatenate([ot[:, :blk], ot[:, blk:]], axis=0)
            o_ref[:, qcol] = (pair_t.T * g_ref[:, qcol].astype(F32)).astype(BF16)


def _swa_kernel(*refs, **kw):
    is_first = pl.program_id(1) == 0

    @pl.when(is_first)
    def _():
        _swa_body(*refs, first=True, **kw)

    @pl.when(jnp.logical_not(is_first))
    def _():
        _swa_body(*refs, first=False, **kw)


def _swa(plain, gates, sinks_l, seq, q_w, kv_w, ret_v_w):
    m = plain.shape[0]
    nb = seq // SWA_BLOCK
    kv_heads = SWA_KV_HEADS
    group = q_w // SWA_HEAD_DIM // kv_heads
    q_blk = ret_v_w // q_w
    k_blk = (ret_v_w + q_w) // kv_w
    v_blk = k_blk + 1
    g_blk = ret_v_w // q_w
    cur = lambda b, n: b * nb + n
    prev = lambda b, n: b * nb + jnp.maximum(n - 1, 0)
    return pl.pallas_call(
        functools.partial(_swa_kernel, kv_heads=kv_heads, group=group, hd=SWA_HEAD_DIM),
        grid=(m // seq, nb),
        in_specs=[
            pl.BlockSpec(memory_space=pltpu.SMEM),
            pl.BlockSpec((SWA_BLOCK, q_w), lambda b, n: (cur(b, n), q_blk)),
            pl.BlockSpec((SWA_BLOCK, kv_w), lambda b, n: (prev(b, n), k_blk)),
            pl.BlockSpec((SWA_BLOCK, kv_w), lambda b, n: (cur(b, n), k_blk)),
            pl.BlockSpec((SWA_BLOCK, kv_w), lambda b, n: (prev(b, n), v_blk)),
            pl.BlockSpec((SWA_BLOCK, kv_w), lambda b, n: (cur(b, n), v_blk)),
            pl.BlockSpec((SWA_BLOCK, q_w), lambda b, n: (cur(b, n), g_blk)),
        ],
        out_specs=pl.BlockSpec((SWA_BLOCK, q_w), lambda b, n: (cur(b, n), 0)),
        out_shape=jax.ShapeDtypeStruct((m, q_w), BF16),
        compiler_params=_params(("parallel", "arbitrary")),
        name="swa",
    )(sinks_l, plain, plain, plain, plain, plain, gates)


def _outproj_kernel(*refs, last):
    if last:
        r_ref, a_ref, sg_ref, h_ref, gate_ref, wr_ref, ws_ref, wo_ref, fw_ref, out_ref = refs
    else:
        (r_ref, a_ref, sg_ref, h_ref, gate_ref, wr_ref, ws_ref, wo_ref,
         nw_ref, sc_ref, sh_ref, hn_ref, u_ref) = refs
    d = h_ref.shape[1]
    y1 = jnp.dot(r_ref[...], wr_ref[...], preferred_element_type=F32)
    y2 = jnp.dot(a_ref[...], ws_ref[...], preferred_element_type=F32)
    merged = sg_ref[:, :d].astype(F32) * y1 + sg_ref[:, d:].astype(F32) * y2
    z = jnp.dot(merged.astype(BF16), wo_ref[...], preferred_element_type=F32)
    hn = h_ref[...] + gate_ref[...] * z
    if last:
        ms = jnp.mean(hn * hn, axis=-1, keepdims=True)
        out_ref[...] = hn * lax.rsqrt(ms + EPS) * fw_ref[...]
    else:
        hn_ref[...] = hn
        u_ref[...] = _modulated_norm(hn, nw_ref[...], sc_ref[...], sh_ref[...]).astype(BF16)


def _outproj(r, a, sig, h, gate, wr, ws, wo, layer, seq, tm, nxt=None, final_w=None):
    m, d = h.shape
    per_batch = seq // tm
    last = nxt is None
    row = lambda i: (i, 0)
    vec = pl.BlockSpec((None, 1, d), lambda i: (i // per_batch, 0, 0))
    wspec = pl.BlockSpec((None, d, d), lambda i: (layer, 0, 0), pipeline_mode=pl.Buffered(1))
    in_specs = [
        pl.BlockSpec((tm, d), row), pl.BlockSpec((tm, d), row), pl.BlockSpec((tm, 2 * d), row),
        pl.BlockSpec((tm, d), row), vec, wspec, wspec, wspec,
    ]
    args = [r, a, sig, h, gate, wr, ws, wo]
    if last:
        in_specs.append(pl.BlockSpec((1, d), lambda i: (0, 0)))
        args.append(final_w)
        out_specs = pl.BlockSpec((tm, d), row)
        out_shape = jax.ShapeDtypeStruct((m, d), F32)
    else:
        nw, sc, sh = nxt
        in_specs += [pl.BlockSpec((1, d), lambda i: (0, 0)), vec, vec]
        args += [nw, sc, sh]
        out_specs = (pl.BlockSpec((tm, d), row), pl.BlockSpec((tm, d), row))
        out_shape = (jax.ShapeDtypeStruct((m, d), F32), jax.ShapeDtypeStruct((m, d), BF16))
    return pl.pallas_call(
        functools.partial(_outproj_kernel, last=last),
        grid=(m // tm,),
        in_specs=in_specs,
        out_specs=out_specs,
        out_shape=out_shape,
        compiler_params=_params(("parallel",)),
        name="outproj_last" if last else "outproj",
    )(*args)


def kernel(x, c, norm_w, ada_w, ada_b, w_in, ret_gn_w, attn_sinks, w_ret_o, w_swa_o, w_out, final_norm_w):
    b, s, d = x.shape
    depth = norm_w.shape[0]
    m = b * s
    ret_v_w = ret_gn_w.shape[1]
    ret_qk_w = ret_v_w // 2
    swa_q_w = d
    swa_kv_w = SWA_KV_HEADS * SWA_HEAD_DIM
    tn = 1024
    tm = 1024
    assert w_in.shape[2] == 2 * ret_qk_w + 2 * ret_v_w + 2 * swa_q_w + 2 * swa_kv_w + 2 * d
    assert ret_qk_w == tn and 2 * swa_kv_w == tn and s % tm == 0

    c_pad = jnp.pad(c, ((0, 8 - b), (0, 0)))
    mod = _modulation(c_pad, ada_w, ada_b.reshape(depth, 1, 3 * d), tn=1536)[:, :b]
    shift = mod[:, :, :d].reshape(depth, b, 1, d)
    scale = mod[:, :, d:2 * d].reshape(depth, b, 1, d)
    gate = mod[:, :, 2 * d:].reshape(depth, b, 1, d)

    half = ret_qk_w // RET_HEADS // 2
    inv = 1.0 / (RET_ROT_BASE ** jnp.linspace(0.0, 1.0, half, dtype=F32))
    ang = jnp.arange(s, dtype=F32)[:, None] * inv[None, :]
    cos2 = jnp.concatenate([jnp.cos(ang), jnp.cos(ang)], axis=-1)
    sin2 = jnp.concatenate([-jnp.sin(ang), jnp.sin(ang)], axis=-1)
    k_scale = float(2 * half) ** -0.5
    cos_tab = jnp.stack([cos2, cos2 * k_scale])
    sin_tab = jnp.stack([sin2, sin2 * k_scale])

    wr_bf = w_ret_o.astype(BF16)
    ws_bf = w_swa_o.astype(BF16)
    wo_bf = w_out.astype(BF16)
    gn_w3 = ret_gn_w.reshape(depth, 1, ret_v_w)
    nw2 = norm_w.reshape(depth, 1, d)

    h = x.reshape(m, d)
    u = _first_u(h, nw2[0], scale[0], shift[0], s, tm=512)

    for l in range(depth):
        rqk = _inproj(u, w_in, l, lambda j: j, 2, "rot", tm, tn, s, tables=(cos_tab, sin_tab))
        plain = _inproj(u, w_in, l, lambda j: jnp.where(j < 2, j + 2, j + 4), 5, "plain", tm, tn, s)
        gates = _inproj(u, w_in, l, lambda j: jnp.where(j < 2, j + 4, j + 7), 4, "silu", tm, tn, s)
        sig = _inproj(u, w_in, l, lambda j: j + 11, 4, "sigmoid", tm, tn, s)
        r = _retention(rqk, plain, gates, gn_w3, l, s, t=512)
        a = _swa(plain, gates, attn_sinks[l], s, swa_q_w, swa_kv_w, ret_v_w)
        if l + 1 < depth:
            h, u = _outproj(r, a, sig, h, gate[l], wr_bf, ws_bf, wo_bf, l, s, tm=256,
                            nxt=(nw2[l + 1], scale[l + 1], shift[l + 1]))
        else:
            h = _outproj(r, a, sig, h, gate[l], wr_bf, ws_bf, wo_bf, l, s, tm=256,
                         final_w=final_norm_w.reshape(1, d))
    return h.reshape(b, s, d)
```

```python
import functools

import numpy as np
import jax
import jax.numpy as jnp
from jax import lax
from jax.experimental import pallas as pl
from jax.experimental.pallas import tpu as pltpu

F32 = jnp.float32
BF16 = jnp.bfloat16

RET_HEADS = 8
RET_ROT_BASE = 10000.0
SWA_HEAD_DIM = 64
SWA_KV_HEADS = 8
SWA_BLOCK = 128
EPS = 1e-6

RET_CHUNK = 256

V7X_VMEM_BYTES = 64 * 1024 * 1024
VMEM_LIMIT_BYTES = V7X_VMEM_BYTES - 8 * 1024 * 1024

NEG_BIG = -1e30


def _params(semantics):
    return pltpu.CompilerParams(dimension_semantics=semantics, vmem_limit_bytes=VMEM_LIMIT_BYTES)


def _sigmoid(y):
    return 1.0 / (1.0 + jnp.exp(-y))


def _mod_kernel(ct_ref, w_ref, b_ref, o_ref, *, batch):
    ct = ct_ref[...]
    ca = ct * _sigmoid(ct)
    w = w_ref[...]
    for b in range(batch):
        o_ref[b:b + 1, :] = jnp.sum(w * ca[:, b:b + 1], axis=0, keepdims=True) + b_ref[...]


def _modulation(c_t, ada_w, ada_b3, batch, tn):
    depth, d, n3 = ada_w.shape
    cols = c_t.shape[1]
    return pl.pallas_call(
        functools.partial(_mod_kernel, batch=batch),
        grid=(depth, n3 // tn),
        in_specs=[
            pl.BlockSpec((d, cols), lambda l, j: (0, 0)),
            pl.BlockSpec((None, d, tn), lambda l, j: (l, 0, j)),
            pl.BlockSpec((None, 1, tn), lambda l, j: (l, 0, j)),
        ],
        out_specs=pl.BlockSpec((None, batch, tn), lambda l, j: (l, 0, j)),
        out_shape=jax.ShapeDtypeStruct((depth, batch, n3), F32),
        compiler_params=_params(("parallel", "parallel")),
        name="adaln_modulation",
    )(c_t, ada_w, ada_b3)


def _modulated_norm(h, nw, scale, shift):
    ms = jnp.mean(h * h, axis=-1, keepdims=True)
    y = h * lax.rsqrt(ms + EPS) * nw
    return y * (1.0 + scale) + shift


def _u_kernel(x_ref, nw_ref, sc_ref, sh_ref, u_ref):
    u_ref[...] = _modulated_norm(x_ref[...], nw_ref[...], sc_ref[...], sh_ref[...]).astype(BF16)


def _first_u(x2, nw, scale, shift, seq, tm):
    m, d = x2.shape
    per_batch = seq // tm
    return pl.pallas_call(
        _u_kernel,
        grid=(m // tm,),
        in_specs=[
            pl.BlockSpec((tm, d), lambda i: (i, 0)),
            pl.BlockSpec((1, d), lambda i: (0, 0)),
            pl.BlockSpec((None, 1, d), lambda i: (i // per_batch, 0, 0)),
            pl.BlockSpec((None, 1, d), lambda i: (i // per_batch, 0, 0)),
        ],
        out_specs=pl.BlockSpec((tm, d), lambda i: (i, 0)),
        out_shape=jax.ShapeDtypeStruct((m, d), BF16),
        compiler_params=_params(("parallel",)),
        name="first_modulated_norm",
    )(x2, nw, scale, shift)


def _inproj_rot_kernel(u_ref, w_ref, cos_ref, sin_ref, o_ref, wbf_ref, *, halves):
    @pl.when(pl.program_id(1) == 0)
    def _():
        wbf_ref[...] = w_ref[...].astype(BF16)

    hd = cos_ref.shape[-1]
    sub = u_ref.shape[0] // halves
    for part in range(halves):
        rows = slice(part * sub, (part + 1) * sub)
        y = jnp.dot(u_ref[rows, :], wbf_ref[...], preferred_element_type=F32)
        cs = cos_ref[rows, :]
        sn = sin_ref[rows, :]
        for h in range(y.shape[1] // hd):
            yh = y[:, h * hd:(h + 1) * hd]
            o_ref[rows, h * hd:(h + 1) * hd] = (yh * cs + pltpu.roll(yh, hd // 2, 1) * sn).astype(BF16)


def _inproj_gen_kernel(u_ref, w_ref, o_ref, wbf_ref, *, halves, silu_tiles, sigmoid_tiles):
    @pl.when(pl.program_id(1) == 0)
    def _():
        wbf_ref[...] = w_ref[...].astype(BF16)

    j = pl.program_id(0)
    is_silu = jnp.logical_and(j >= silu_tiles[0], j < silu_tiles[1])
    is_sig = jnp.logical_and(j >= sigmoid_tiles[0], j < sigmoid_tiles[1])
    sub = u_ref.shape[0] // halves

    def body(epilogue):
        for part in range(halves):
            rows = slice(part * sub, (part + 1) * sub)
            y = jnp.dot(u_ref[rows, :], wbf_ref[...], preferred_element_type=F32)
            o_ref[rows, :] = epilogue(y).astype(BF16)

    @pl.when(is_silu)
    def _():
        body(lambda y: y * _sigmoid(y))

    @pl.when(is_sig)
    def _():
        body(_sigmoid)

    @pl.when(jnp.logical_not(jnp.logical_or(is_silu, is_sig)))
    def _():
        body(lambda y: y)


def _inproj(kernel_fn, u, w_in, layer, col_map, n_tiles, tm, tn, name, extra_specs=(), extra_args=()):
    m, d = u.shape
    return pl.pallas_call(
        kernel_fn,
        grid=(n_tiles, m // tm),
        in_specs=[
            pl.BlockSpec((tm, d), lambda j, i: (i, 0)),
            pl.BlockSpec((None, d, tn), lambda j, i: (layer, 0, col_map(j))),
        ] + list(extra_specs),
        out_specs=pl.BlockSpec((tm, tn), lambda j, i: (i, j)),
        out_shape=jax.ShapeDtypeStruct((m, n_tiles * tn), BF16),
        scratch_shapes=[pltpu.VMEM((d, tn), BF16)],
        compiler_params=_params(("arbitrary", "arbitrary")),
        name=name,
    )(u, w_in, *extra_args)


def _retention_kernel(qk_ref, v_ref, g_ref, gnw_ref, o_ref, state_ref, *, heads, chunk, log_gammas):
    @pl.when(pl.program_id(1) == 0)
    def _():
        state_ref[...] = jnp.zeros_like(state_ref)

    t = qk_ref.shape[0]
    dk = qk_ref.shape[1] // (2 * heads)
    dv = v_ref.shape[1] // heads
    row = lax.broadcasted_iota(jnp.int32, (chunk, chunk), 0)
    col = lax.broadcasted_iota(jnp.int32, (chunk, chunk), 1)
    causal = row >= col
    pos = lax.broadcasted_iota(jnp.int32, (chunk, 1), 0).astype(F32)

    for h in range(heads):
        lg = log_gammas[h]
        q_scale = jnp.exp(lg * pos)
        k_scale = jnp.exp(-lg * pos)
        g_chunk = float(np.exp(lg * chunk))
        gnw = gnw_ref[:, h * dv:(h + 1) * dv]
        carry = state_ref[h]
        for c in range(t // chunk):
            rows = slice(c * chunk, (c + 1) * chunk)
            qa = (qk_ref[rows, h * dk:(h + 1) * dk].astype(F32) * q_scale).astype(BF16)
            kb = (qk_ref[rows, (heads + h) * dk:(heads + h + 1) * dk].astype(F32) * k_scale).astype(BF16)
            v = v_ref[rows, h * dv:(h + 1) * dv]
            s = lax.dot_general(qa, kb, (((1,), (1,)), ((), ())), preferred_element_type=F32)
            lhs = jnp.concatenate([jnp.where(causal, s, 0.0).astype(BF16), qa], axis=1)
            rhs = jnp.concatenate([v, carry.astype(BF16)], axis=0)
            o = jnp.dot(lhs, rhs, preferred_element_type=F32)
            kv = lax.dot_general(kb, v, (((0,), (0,)), ((), ())), preferred_element_type=F32)
            carry = (carry + kv) * g_chunk
            mu = jnp.mean(o, axis=-1, keepdims=True)
            dlt = o - mu
            var = jnp.mean(dlt * dlt, axis=-1, keepdims=True)
            rn = dlt * lax.rsqrt(var + EPS) * gnw
            o_ref[rows, h * dv:(h + 1) * dv] = (rn * g_ref[rows, h * dv:(h + 1) * dv].astype(F32)).astype(BF16)
        state_ref[h] = carry


def _retention(rqk, proj, gn_w3, layer, seq, t, gate_blk):
    m, qk_w = rqk.shape
    v_w = gn_w3.shape[-1]
    per_batch = seq // t
    log_gammas = tuple(float(np.log1p(-np.exp2(-5.0 - h))) for h in range(RET_HEADS))
    dk = qk_w // (2 * RET_HEADS)
    dv = v_w // RET_HEADS
    return pl.pallas_call(
        functools.partial(_retention_kernel, heads=RET_HEADS, chunk=RET_CHUNK, log_gammas=log_gammas),
        grid=(m // seq, per_batch),
        in_specs=[
            pl.BlockSpec((t, qk_w), lambda b, s: (b * per_batch + s, 0)),
            pl.BlockSpec((t, v_w), lambda b, s: (b * per_batch + s, 0)),
            pl.BlockSpec((t, v_w), lambda b, s: (b * per_batch + s, gate_blk)),
            pl.BlockSpec((None, 1, v_w), lambda b, s: (layer, 0, 0)),
        ],
        out_specs=pl.BlockSpec((t, v_w), lambda b, s: (b * per_batch + s, 0)),
        out_shape=jax.ShapeDtypeStruct((m, v_w), BF16),
        scratch_shapes=[pltpu.VMEM((RET_HEADS, dk, dv), F32)],
        compiler_params=_params(("parallel", "arbitrary")),
        name="retention",
    )(rqk, proj, proj, gn_w3)


def _swa_body(sink_ref, q_ref, kp_ref, kc_ref, vp_ref, vc_ref, g_ref, o_ref, *, kv_heads, group, hd, first):
    blk = q_ref.shape[0]
    lanes = 2 * hd
    log2e = float(np.log2(np.e))
    lane = lax.broadcasted_iota(jnp.int32, (1, lanes), 1)
    lo_f = lane < hd
    scale = float(hd) ** -0.5 * log2e
    qmask = (jnp.where(lo_f, scale, 0.0).astype(BF16), jnp.where(lo_f, 0.0, scale).astype(BF16))
    key = lax.broadcasted_iota(jnp.int32, (blk, 2 * blk), 0)
    qry = jnp.bitwise_and(lax.broadcasted_iota(jnp.int32, (blk, 2 * blk), 1), blk - 1)
    upper = key > qry
    head_lo = lax.broadcasted_iota(jnp.int32, (1, 2 * blk), 1) < blk

    def dup(tile_bf16, parity):
        tile = tile_bf16.astype(F32)
        rolled = pltpu.roll(tile, hd, 1)
        keep_lo = lo_f if parity == 0 else jnp.logical_not(lo_f)
        return jnp.where(keep_lo, tile, rolled)

    for h in range(kv_heads):
        tcol = slice((h // 2) * lanes, (h // 2 + 1) * lanes)
        par = h % 2
        hrows = slice(par * hd, (par + 1) * hd)
        k_cur = dup(kc_ref[:, tcol], par).astype(BF16)
        if par == 0:
            vt_tile_cur = vc_ref[:, tcol].astype(F32).T.astype(BF16)
        vt_cur = vt_tile_cur
        if not first:
            k_prev = dup(kp_ref[:, tcol], par).astype(BF16)
            if par == 0:
                vt_tile_prev = vp_ref[:, tcol].astype(F32).T.astype(BF16)
            vt_prev = vt_tile_prev
        for gp in range(group // 2):
            qt = (h * group) // 2 + gp
            qcol = slice(qt * lanes, (qt + 1) * lanes)
            qtile = q_ref[:, qcol]
            qq = jnp.concatenate([qtile * qmask[0], qtile * qmask[1]], axis=0)
            nt = (((1,), (1,)), ((), ()))
            s_cur = lax.dot_general(k_cur, qq, nt, preferred_element_type=F32)
            if first:
                s = jnp.where(upper, NEG_BIG, s_cur)
            else:
                s = jnp.where(upper, lax.dot_general(k_prev, qq, nt, preferred_element_type=F32), s_cur)
            hq = h * group + 2 * gp
            sink = jnp.where(head_lo, sink_ref[hq], sink_ref[hq + 1]) * log2e
            mx = jnp.maximum(jnp.max(s, axis=0, keepdims=True), sink)
            p = jnp.exp2(s - mx)
            inv = 1.0 / (jnp.sum(p, axis=0, keepdims=True) + jnp.exp2(sink - mx))
            if first:
                ot = jnp.dot(vt_cur, p.astype(BF16), preferred_element_type=F32)
            else:
                ot = (jnp.dot(vt_prev, jnp.where(upper, p, 0.0).astype(BF16), preferred_element_type=F32)
                      + jnp.dot(vt_cur, jnp.where(upper, 0.0, p).astype(BF16), preferred_element_type=F32))
            ot = ot[hrows] * inv
            pair_t = jnp.concatenate([ot[:, :blk], ot[:, blk:]], axis=0)
            o_ref[:, qcol] = (pair_t.T * g_ref[:, qcol].astype(F32)).astype(BF16)


def _swa_kernel(*refs, **kw):
    is_first = pl.program_id(1) == 0

    @pl.when(is_first)
    def _():
        _swa_body(*refs, first=True, **kw)

    @pl.when(jnp.logical_not(is_first))
    def _():
        _swa_body(*refs, first=False, **kw)


def _swa(proj, sinks_l, seq, q_w, kv_w, q_blk, k_blk, v_blk, g_blk):
    m = proj.shape[0]
    nb = seq // SWA_BLOCK
    kv_heads = SWA_KV_HEADS
    group = q_w // SWA_HEAD_DIM // kv_heads
    cur = lambda b, n: b * nb + n
    prev = lambda b, n: b * nb + jnp.maximum(n - 1, 0)
    return pl.pallas_call(
        functools.partial(_swa_kernel, kv_heads=kv_heads, group=group, hd=SWA_HEAD_DIM),
        grid=(m // seq, nb),
        in_specs=[
            pl.BlockSpec(memory_space=pltpu.SMEM),
            pl.BlockSpec((SWA_BLOCK, q_w), lambda b, n: (cur(b, n), q_blk)),
            pl.BlockSpec((SWA_BLOCK, kv_w), lambda b, n: (prev(b, n), k_blk)),
            pl.BlockSpec((SWA_BLOCK, kv_w), lambda b, n: (cur(b, n), k_blk)),
            pl.BlockSpec((SWA_BLOCK, kv_w), lambda b, n: (prev(b, n), v_blk)),
            pl.BlockSpec((SWA_BLOCK, kv_w), lambda b, n: (cur(b, n), v_blk)),
            pl.BlockSpec((SWA_BLOCK, q_w), lambda b, n: (cur(b, n), g_blk)),
        ],
        out_specs=pl.BlockSpec((SWA_BLOCK, q_w), lambda b, n: (cur(b, n), 0)),
        out_shape=jax.ShapeDtypeStruct((m, q_w), BF16),
        compiler_params=_params(("parallel", "arbitrary")),
        name="swa",
    )(sinks_l, proj, proj, proj, proj, proj, proj)


def _outproj_kernel(*refs, last):
    if last:
        r_ref, a_ref, sg_ref, h_ref, gate_ref, wr_ref, ws_ref, wo_ref, fw_ref, out_ref = refs
    else:
        (r_ref, a_ref, sg_ref, h_ref, gate_ref, wr_ref, ws_ref, wo_ref,
         nw_ref, sc_ref, sh_ref, hn_ref, u_ref) = refs
    d = h_ref.shape[1]
    y1 = jnp.dot(r_ref[...], wr_ref[...], preferred_element_type=F32)
    y2 = jnp.dot(a_ref[...], ws_ref[...], preferred_element_type=F32)
    merged = sg_ref[:, :d].astype(F32) * y1 + sg_ref[:, d:].astype(F32) * y2
    z = jnp.dot(merged.astype(BF16), wo_ref[...], preferred_element_type=F32)
    hn = h_ref[...] + gate_ref[...] * z
    if last:
        ms = jnp.mean(hn * hn, axis=-1, keepdims=True)
        out_ref[...] = hn * lax.rsqrt(ms + EPS) * fw_ref[...]
    else:
        hn_ref[...] = hn
        u_ref[...] = _modulated_norm(hn, nw_ref[...], sc_ref[...], sh_ref[...]).astype(BF16)


def _outproj(r, a, proj, sig_blk, h, gate, wr, ws, wo, layer, seq, tm, nxt=None, final_w=None):
    m, d = h.shape
    per_batch = seq // tm
    last = nxt is None
    row = lambda i: (i, 0)
    vec = pl.BlockSpec((None, 1, d), lambda i: (i // per_batch, 0, 0))
    wspec = lambda rows: pl.BlockSpec((None, rows, d), lambda i: (layer, 0, 0),
                                      pipeline_mode=pl.Buffered(1))
    in_specs = [
        pl.BlockSpec((tm, r.shape[1]), row), pl.BlockSpec((tm, d), row),
        pl.BlockSpec((tm, 2 * d), lambda i: (i, sig_blk)),
        pl.BlockSpec((tm, d), row), vec, wspec(r.shape[1]), wspec(d), wspec(d),
    ]
    args = [r, a, proj, h, gate, wr, ws, wo]
    if last:
        in_specs.append(pl.BlockSpec((1, d), lambda i: (0, 0)))
        args.append(final_w)
        out_specs = pl.BlockSpec((tm, d), row)
        out_shape = jax.ShapeDtypeStruct((m, d), F32)
    else:
        nw, sc, sh = nxt
        in_specs += [pl.BlockSpec((1, d), lambda i: (0, 0)), vec, vec]
        args += [nw, sc, sh]
        out_specs = (pl.BlockSpec((tm, d), row), pl.BlockSpec((tm, d), row))
        out_shape = (jax.ShapeDtypeStruct((m, d), F32), jax.ShapeDtypeStruct((m, d), BF16))
    return pl.pallas_call(
        functools.partial(_outproj_kernel, last=last),
        grid=(m // tm,),
        in_specs=in_specs,
        out_specs=out_specs,
        out_shape=out_shape,
        compiler_params=_params(("parallel",)),
        name="outproj_last" if last else "outproj",
    )(*args)


def kernel(x, c, norm_w, ada_w, ada_b, w_in, ret_gn_w, attn_sinks, w_ret_o, w_swa_o, w_out, final_norm_w):
    b, s, d = x.shape
    depth = norm_w.shape[0]
    m = b * s
    ret_v_w = ret_gn_w.shape[1]
    ret_qk_w = ret_v_w // 2
    swa_q_w = d
    swa_kv_w = SWA_KV_HEADS * SWA_HEAD_DIM
    tn = 1024
    tm = 2048
    assert w_in.shape[2] == 2 * ret_qk_w + 2 * ret_v_w + 2 * swa_q_w + 2 * swa_kv_w + 2 * d
    assert ret_qk_w == tn and 2 * swa_kv_w == tn and ret_v_w == 2 * tn and d == 2 * tn and s % tm == 0

    c_t = jnp.pad(c.T, ((0, 0), (0, 8 - b)))
    mod = _modulation(c_t, ada_w, ada_b.reshape(depth, 1, 3 * d), b, tn=1536)
    shift = mod[:, :, :d].reshape(depth, b, 1, d)
    scale = mod[:, :, d:2 * d].reshape(depth, b, 1, d)
    gate = mod[:, :, 2 * d:].reshape(depth, b, 1, d)

    half = ret_qk_w // RET_HEADS // 2
    inv = 1.0 / (RET_ROT_BASE ** jnp.linspace(0.0, 1.0, half, dtype=F32))
    ang = jnp.arange(s, dtype=F32)[:, None] * inv[None, :]
    cos2 = jnp.concatenate([jnp.cos(ang), jnp.cos(ang)], axis=-1)
    sin2 = jnp.concatenate([-jnp.sin(ang), jnp.sin(ang)], axis=-1)
    k_scale = float(2 * half) ** -0.5
    cos_tab = jnp.stack([cos2, cos2 * k_scale])
    sin_tab = jnp.stack([sin2, sin2 * k_scale])

    wr_bf = w_ret_o.astype(BF16)
    ws_bf = w_swa_o.astype(BF16)
    wo_bf = w_out.astype(BF16)
    gn_w3 = ret_gn_w.reshape(depth, 1, ret_v_w)
    nw2 = norm_w.reshape(depth, 1, d)

    h = x.reshape(m, d)
    u = _first_u(h, nw2[0], scale[0], shift[0], s, tm=512)

    gen_map = lambda j: jnp.where(j < 2, j + 2, jnp.where(j < 4, j + 4, jnp.where(j < 6, j,
                                  jnp.where(j < 12, j + 3, 8))))
    gen_kernel = functools.partial(_inproj_gen_kernel, halves=2, silu_tiles=(4, 8), sigmoid_tiles=(8, 12))
    rot_kernel = functools.partial(_inproj_rot_kernel, halves=2)
    per_batch = s // tm
    hd2 = cos_tab.shape[-1]
    tab_spec = pl.BlockSpec((None, tm, hd2), lambda j, i: (j, i % per_batch, 0))

    for l in range(depth):
        rqk = _inproj(rot_kernel, u, w_in, l, lambda j: j, 2, tm, tn, "inproj_rot",
                      extra_specs=(tab_spec, tab_spec), extra_args=(cos_tab, sin_tab))
        proj = _inproj(gen_kernel, u, w_in, l, gen_map, 13, tm, tn, "inproj_gen")
        r = _retention(rqk, proj, gn_w3, l, s, t=512, gate_blk=2)
        a = _swa(proj, attn_sinks[l], s, swa_q_w, swa_kv_w, q_blk=1, k_blk=24, v_blk=25, g_blk=3)
        if l + 1 < depth:
            h, u = _outproj(r, a, proj, 2, h, gate[l], wr_bf, ws_bf, wo_bf, l, s, tm=256,
                            nxt=(nw2[l + 1], scale[l + 1], shift[l + 1]))
        else:
            h = _outproj(r, a, proj, 2, h, gate[l], wr_bf, ws_bf, wo_bf, l, s, tm=256,
                         final_w=final_norm_w.reshape(1, d))
    return h.reshape(b, s, d)
```

```python
import functools

import numpy as np
import jax
import jax.numpy as jnp
from jax import lax
from jax.experimental import pallas as pl
from jax.experimental.pallas import tpu as pltpu

F32 = jnp.float32
BF16 = jnp.bfloat16

RET_HEADS = 8
RET_ROT_BASE = 10000.0
SWA_HEAD_DIM = 64
SWA_KV_HEADS = 8
SWA_BLOCK = 128
EPS = 1e-6

RET_CHUNK = 256

V7X_VMEM_BYTES = 64 * 1024 * 1024
VMEM_LIMIT_BYTES = V7X_VMEM_BYTES - 6 * 1024 * 1024

NEG_BIG = -1e30


def _params(semantics):
    return pltpu.CompilerParams(dimension_semantics=semantics, vmem_limit_bytes=VMEM_LIMIT_BYTES)


def _sigmoid(y):
    return 0.5 * jnp.tanh(0.5 * y) + 0.5


def _mod_kernel(ct_ref, w_ref, b_ref, o_ref, *, batch):
    ct = ct_ref[...]
    ca = ct * _sigmoid(ct)
    w = w_ref[...]
    for b in range(batch):
        o_ref[b:b + 1, :] = jnp.sum(w * ca[:, b:b + 1], axis=0, keepdims=True) + b_ref[...]


def _modulation(c_t, ada_w, ada_b3, batch, tn):
    depth, d, n3 = ada_w.shape
    cols = c_t.shape[1]
    return pl.pallas_call(
        functools.partial(_mod_kernel, batch=batch),
        grid=(depth, n3 // tn),
        in_specs=[
            pl.BlockSpec((d, cols), lambda l, j: (0, 0)),
            pl.BlockSpec((None, d, tn), lambda l, j: (l, 0, j)),
            pl.BlockSpec((None, 1, tn), lambda l, j: (l, 0, j)),
        ],
        out_specs=pl.BlockSpec((None, batch, tn), lambda l, j: (l, 0, j)),
        out_shape=jax.ShapeDtypeStruct((depth, batch, n3), F32),
        compiler_params=_params(("parallel", "parallel")),
        name="adaln_modulation",
    )(c_t, ada_w, ada_b3)


def _modulated_norm(h, nw, scale, shift):
    ms = jnp.mean(h * h, axis=-1, keepdims=True)
    y = h * lax.rsqrt(ms + EPS) * nw
    return y * (1.0 + scale) + shift


def _u_kernel(x_ref, nw_ref, sc_ref, sh_ref, u_ref):
    u_ref[...] = _modulated_norm(x_ref[...], nw_ref[...], sc_ref[...], sh_ref[...]).astype(BF16)


def _first_u(x2, nw, scale, shift, seq, tm):
    m, d = x2.shape
    per_batch = seq // tm
    return pl.pallas_call(
        _u_kernel,
        grid=(m // tm,),
        in_specs=[
            pl.BlockSpec((tm, d), lambda i: (i, 0)),
            pl.BlockSpec((1, d), lambda i: (0, 0)),
            pl.BlockSpec((None, 1, d), lambda i: (i // per_batch, 0, 0)),
            pl.BlockSpec((None, 1, d), lambda i: (i // per_batch, 0, 0)),
        ],
        out_specs=pl.BlockSpec((tm, d), lambda i: (i, 0)),
        out_shape=jax.ShapeDtypeStruct((m, d), BF16),
        compiler_params=_params(("parallel",)),
        name="first_modulated_norm",
    )(x2, nw, scale, shift)


def _inproj_rot_kernel(u_ref, w_ref, cos_ref, sin_ref, o_ref, wbf_ref, *, halves):
    @pl.when(pl.program_id(1) == 0)
    def _():
        wbf_ref[...] = w_ref[...].astype(BF16)

    hd = cos_ref.shape[-1]
    sub = u_ref.shape[0] // halves
    for part in range(halves):
        rows = slice(part * sub, (part + 1) * sub)
        y = jnp.dot(u_ref[rows, :], wbf_ref[...], preferred_element_type=F32)
        cs = cos_ref[rows, :]
        sn = sin_ref[rows, :]
        for h in range(y.shape[1] // hd):
            yh = y[:, h * hd:(h + 1) * hd]
            o_ref[rows, h * hd:(h + 1) * hd] = (yh * cs + pltpu.roll(yh, hd // 2, 1) * sn).astype(BF16)


def _inproj_gen_kernel(*refs, halves, silu_tiles, sigmoid_tiles, n_cast):
    u_ref, w_ref = refs[:2]
    cast_in = refs[2:2 + n_cast]
    o_ref = refs[2 + n_cast]
    cast_out = refs[3 + n_cast:3 + 2 * n_cast]
    wbf_ref = refs[3 + 2 * n_cast]

    @pl.when(pl.program_id(1) == 0)
    def _():
        wbf_ref[...] = w_ref[...].astype(BF16)

    j = pl.program_id(0)
    is_silu = jnp.logical_and(j >= silu_tiles[0], j < silu_tiles[1])
    is_sig = jnp.logical_and(j >= sigmoid_tiles[0], j < sigmoid_tiles[1])
    sub = u_ref.shape[0] // halves

    def body(epilogue):
        for src, dst in zip(cast_in, cast_out):
            dst[...] = src[...].astype(BF16)
        for part in range(halves):
            rows = slice(part * sub, (part + 1) * sub)
            y = jnp.dot(u_ref[rows, :], wbf_ref[...], preferred_element_type=F32)
            o_ref[rows, :] = epilogue(y).astype(BF16)

    @pl.when(is_silu)
    def _():
        body(lambda y: y * _sigmoid(y))

    @pl.when(is_sig)
    def _():
        body(_sigmoid)

    @pl.when(jnp.logical_not(jnp.logical_or(is_silu, is_sig)))
    def _():
        body(lambda y: y)


def _inproj(kernel_fn, u, w_in, layer, col_map, n_tiles, tm, tn, name, extra_specs=(), extra_args=(),
            extra_out_specs=(), extra_out_shapes=()):
    m, d = u.shape
    return pl.pallas_call(
        kernel_fn,
        grid=(n_tiles, m // tm),
        in_specs=[
            pl.BlockSpec((tm, d), lambda j, i: (i, 0)),
            pl.BlockSpec((None, d, tn), lambda j, i: (layer, 0, col_map(j))),
        ] + list(extra_specs),
        out_specs=[pl.BlockSpec((tm, tn), lambda j, i: (i, j))] + list(extra_out_specs),
        out_shape=[jax.ShapeDtypeStruct((m, n_tiles * tn), BF16)] + list(extra_out_shapes),
        scratch_shapes=[pltpu.VMEM((d, tn), BF16)],
        compiler_params=_params(("arbitrary", "arbitrary")),
        name=name,
    )(u, w_in, *extra_args)


def _cast_slab_specs(weights, layer, n_steps, row_tiles):
    n_slabs = 1 << ((n_steps // len(weights)).bit_length() - 1)
    in_specs, out_specs, out_shapes = [], [], []
    for idx, w in enumerate(weights):
        rows, cols = w.shape[1:]
        slab = rows // n_slabs
        assert slab * n_slabs == rows and slab % 16 == 0

        def slab_of(j, i, idx=idx):
            return jnp.clip(j * row_tiles + i - idx * n_slabs, 0, n_slabs - 1)

        in_specs.append(pl.BlockSpec((None, slab, cols), lambda j, i, f=slab_of: (layer, f(j, i), 0)))
        out_specs.append(pl.BlockSpec((slab, cols), lambda j, i, f=slab_of: (f(j, i), 0)))
        out_shapes.append(jax.ShapeDtypeStruct((rows, cols), BF16))
    return in_specs, out_specs, out_shapes


def _retention_kernel(qk_ref, v_ref, g_ref, gnw_ref, o_ref, state_ref, *, heads, chunk, log_gammas):
    @pl.when(pl.program_id(1) == 0)
    def _():
        state_ref[...] = jnp.zeros_like(state_ref)

    t = qk_ref.shape[0]
    dk = qk_ref.shape[1] // (2 * heads)
    dv = v_ref.shape[1] // heads
    row = lax.broadcasted_iota(jnp.int32, (chunk, chunk), 0)
    col = lax.broadcasted_iota(jnp.int32, (chunk, chunk), 1)
    causal = row >= col
    pos = lax.broadcasted_iota(jnp.int32, (chunk, 1), 0).astype(F32)

    for h in range(heads):
        lg = log_gammas[h]
        q_scale = jnp.exp(lg * pos)
        k_scale = jnp.exp(-lg * pos)
        g_chunk = float(np.exp(lg * chunk))
        gnw = gnw_ref[:, h * dv:(h + 1) * dv]
        carry = state_ref[h]
        for c in range(t // chunk):
            rows = slice(c * chunk, (c + 1) * chunk)
            qa = (qk_ref[rows, h * dk:(h + 1) * dk].astype(F32) * q_scale).astype(BF16)
            kb = (qk_ref[rows, (heads + h) * dk:(heads + h + 1) * dk].astype(F32) * k_scale).astype(BF16)
            v = v_ref[rows, h * dv:(h + 1) * dv]
            s = lax.dot_general(qa, kb, (((1,), (1,)), ((), ())), preferred_element_type=F32)
            lhs = jnp.concatenate([jnp.where(causal, s, 0.0).astype(BF16), qa], axis=1)
            rhs = jnp.concatenate([v, carry.astype(BF16)], axis=0)
            o = jnp.dot(lhs, rhs, preferred_element_type=F32)
            kv = lax.dot_general(kb, v, (((0,), (0,)), ((), ())), preferred_element_type=F32)
            carry = (carry + kv) * g_chunk
            mu = jnp.mean(o, axis=-1, keepdims=True)
            dlt = o - mu
            var = jnp.mean(dlt * dlt, axis=-1, keepdims=True)
            rn = dlt * lax.rsqrt(var + EPS) * gnw
            o_ref[rows, h * dv:(h + 1) * dv] = (rn * g_ref[rows, h * dv:(h + 1) * dv].astype(F32)).astype(BF16)
        state_ref[h] = carry


def _retention(rqk, proj, gn_w3, layer, seq, t, gate_blk):
    m, qk_w = rqk.shape
    v_w = gn_w3.shape[-1]
    per_batch = seq // t
    log_gammas = tuple(float(np.log1p(-np.exp2(-5.0 - h))) for h in range(RET_HEADS))
    dk = qk_w // (2 * RET_HEADS)
    dv = v_w // RET_HEADS
    return pl.pallas_call(
        functools.partial(_retention_kernel, heads=RET_HEADS, chunk=RET_CHUNK, log_gammas=log_gammas),
        grid=(m // seq, per_batch),
        in_specs=[
            pl.BlockSpec((t, qk_w), lambda b, s: (b * per_batch + s, 0)),
            pl.BlockSpec((t, v_w), lambda b, s: (b * per_batch + s, 0)),
            pl.BlockSpec((t, v_w), lambda b, s: (b * per_batch + s, gate_blk)),
            pl.BlockSpec((None, 1, v_w), lambda b, s: (layer, 0, 0)),
        ],
        out_specs=pl.BlockSpec((t, v_w), lambda b, s: (b * per_batch + s, 0)),
        out_shape=jax.ShapeDtypeStruct((m, v_w), BF16),
        scratch_shapes=[pltpu.VMEM((RET_HEADS, dk, dv), F32)],
        compiler_params=_params(("parallel", "arbitrary")),
        name="retention",
    )(rqk, proj, proj, gn_w3)


def _swa_body(sink_ref, q_ref, kp_ref, kc_ref, vp_ref, vc_ref, g_ref, o_ref, *, kv_heads, group, hd, blk,
              first):
    lanes = 2 * hd
    log2e = float(np.log2(np.e))
    lane = lax.broadcasted_iota(jnp.int32, (1, lanes), 1)
    lo_f = lane < hd
    scale = float(hd) ** -0.5 * log2e
    qmask = (jnp.where(lo_f, scale, 0.0).astype(BF16), jnp.where(lo_f, 0.0, scale).astype(BF16))
    key = lax.broadcasted_iota(jnp.int32, (blk, 2 * blk), 0)
    qry = jnp.bitwise_and(lax.broadcasted_iota(jnp.int32, (blk, 2 * blk), 1), blk - 1)
    upper = key > qry
    head_lo = lax.broadcasted_iota(jnp.int32, (1, 2 * blk), 1) < blk

    def dup(tile_bf16, parity):
        tile = tile_bf16.astype(F32)
        rolled = pltpu.roll(tile, hd, 1)
        keep_lo = lo_f if parity == 0 else jnp.logical_not(lo_f)
        return jnp.where(keep_lo, tile, rolled)

    def key_value_operands(k_tiles, v_tiles):
        ks = [dup(k_tiles[:, (h // 2) * lanes:(h // 2 + 1) * lanes], h % 2).astype(BF16)
              for h in range(kv_heads)]
        vts = [v_tiles[:, t * lanes:(t + 1) * lanes].astype(F32).T.astype(BF16)
               for t in range(kv_heads // 2)]
        return ks, vts

    nt = (((1,), (1,)), ((), ()))
    prev_ops = None if first else key_value_operands(kp_ref[...], vp_ref[...])
    for bi in range(q_ref.shape[0] // blk):
        rows = slice(bi * blk, (bi + 1) * blk)
        cur_ops = key_value_operands(kc_ref[rows, :], vc_ref[rows, :])
        no_prev = prev_ops is None
        for h in range(kv_heads):
            hrows = slice((h % 2) * hd, (h % 2 + 1) * hd)
            k_cur = cur_ops[0][h]
            vt_cur = cur_ops[1][h // 2]
            for gp in range(group // 2):
                qt = (h * group) // 2 + gp
                qcol = slice(qt * lanes, (qt + 1) * lanes)
                qtile = q_ref[rows, qcol]
                qq = jnp.concatenate([qtile * qmask[0], qtile * qmask[1]], axis=0)
                s_cur = lax.dot_general(k_cur, qq, nt, preferred_element_type=F32)
                if no_prev:
                    s = jnp.where(upper, NEG_BIG, s_cur)
                else:
                    s = jnp.where(upper, lax.dot_general(prev_ops[0][h], qq, nt, preferred_element_type=F32),
                                  s_cur)
                hq = h * group + 2 * gp
                sink = jnp.where(head_lo, sink_ref[hq], sink_ref[hq + 1]) * log2e
                mx = jnp.maximum(jnp.max(s, axis=0, keepdims=True), sink)
                p = jnp.exp2(s - mx)
                inv = 1.0 / (jnp.sum(p, axis=0, keepdims=True) + jnp.exp2(sink - mx))
                if no_prev:
                    ot = jnp.dot(vt_cur, p.astype(BF16), preferred_element_type=F32)
                else:
                    ot = (jnp.dot(prev_ops[1][h // 2], jnp.where(upper, p, 0.0).astype(BF16),
                                  preferred_element_type=F32)
                          + jnp.dot(vt_cur, jnp.where(upper, 0.0, p).astype(BF16), preferred_element_type=F32))
                ot = ot[hrows] * inv
                pair_t = jnp.concatenate([ot[:, :blk], ot[:, blk:]], axis=0)
                o_ref[rows, qcol] = (pair_t.T * g_ref[rows, qcol].astype(F32)).astype(BF16)
        prev_ops = cur_ops


def _swa_kernel(*refs, **kw):
    is_first = pl.program_id(1) == 0

    @pl.when(is_first)
    def _():
        _swa_body(*refs, first=True, **kw)

    @pl.when(jnp.logical_not(is_first))
    def _():
        _swa_body(*refs, first=False, **kw)


def _swa(proj, sinks_l, seq, q_w, kv_w, q_blk, k_blk, v_blk, g_blk, blocks_per_step):
    m = proj.shape[0]
    t = blocks_per_step * SWA_BLOCK
    steps = seq // t
    kv_heads = SWA_KV_HEADS
    group = q_w // SWA_HEAD_DIM // kv_heads
    cur = lambda b, n: b * steps + n
    prev = lambda b, n: (b * steps + n) * blocks_per_step - jnp.minimum(n, 1)
    return pl.pallas_call(
        functools.partial(_swa_kernel, kv_heads=kv_heads, group=group, hd=SWA_HEAD_DIM, blk=SWA_BLOCK),
        grid=(m // seq, steps),
        in_specs=[
            pl.BlockSpec(memory_space=pltpu.SMEM),
            pl.BlockSpec((t, q_w), lambda b, n: (cur(b, n), q_blk)),
            pl.BlockSpec((SWA_BLOCK, kv_w), lambda b, n: (prev(b, n), k_blk)),
            pl.BlockSpec((t, kv_w), lambda b, n: (cur(b, n), k_blk)),
            pl.BlockSpec((SWA_BLOCK, kv_w), lambda b, n: (prev(b, n), v_blk)),
            pl.BlockSpec((t, kv_w), lambda b, n: (cur(b, n), v_blk)),
            pl.BlockSpec((t, q_w), lambda b, n: (cur(b, n), g_blk)),
        ],
        out_specs=pl.BlockSpec((t, q_w), lambda b, n: (cur(b, n), 0)),
        out_shape=jax.ShapeDtypeStruct((m, q_w), BF16),
        compiler_params=_params(("parallel", "arbitrary")),
        name="swa",
    )(sinks_l, proj, proj, proj, proj, proj, proj)


def _outproj_kernel(*refs, last):
    if last:
        r_ref, a_ref, sg_ref, h_ref, gate_ref, wr_ref, ws_ref, wo_ref, fw_ref, out_ref = refs
    else:
        (r_ref, a_ref, sg_ref, h_ref, gate_ref, wr_ref, ws_ref, wo_ref,
         nw_ref, sc_ref, sh_ref, hn_ref, u_ref) = refs
    d = h_ref.shape[1]
    y1 = jnp.dot(r_ref[...], wr_ref[...], preferred_element_type=F32)
    y2 = jnp.dot(a_ref[...], ws_ref[...], preferred_element_type=F32)
    merged = sg_ref[:, :d].astype(F32) * y1 + sg_ref[:, d:].astype(F32) * y2
    z = jnp.dot(merged.astype(BF16), wo_ref[...], preferred_element_type=F32)
    hn = h_ref[...] + gate_ref[...] * z
    if last:
        ms = jnp.mean(hn * hn, axis=-1, keepdims=True)
        out_ref[...] = hn * lax.rsqrt(ms + EPS) * fw_ref[...]
    else:
        hn_ref[...] = hn
        u_ref[...] = _modulated_norm(hn, nw_ref[...], sc_ref[...], sh_ref[...]).astype(BF16)


def _outproj(r, a, proj, sig_blk, h, gate, wr, ws, wo, seq, tm, nxt=None, final_w=None):
    m, d = h.shape
    per_batch = seq // tm
    last = nxt is None
    row = lambda i: (i, 0)
    vec = pl.BlockSpec((None, 1, d), lambda i: (i // per_batch, 0, 0))
    wspec = lambda rows: pl.BlockSpec((rows, d), lambda i: (0, 0), pipeline_mode=pl.Buffered(1))
    in_specs = [
        pl.BlockSpec((tm, r.shape[1]), row), pl.BlockSpec((tm, d), row),
        pl.BlockSpec((tm, 2 * d), lambda i: (i, sig_blk)),
        pl.BlockSpec((tm, d), row), vec, wspec(r.shape[1]), wspec(d), wspec(d),
    ]
    args = [r, a, proj, h, gate, wr, ws, wo]
    if last:
        in_specs.append(pl.BlockSpec((1, d), lambda i: (0, 0)))
        args.append(final_w)
        out_specs = pl.BlockSpec((tm, d), row)
        out_shape = jax.ShapeDtypeStruct((m, d), F32)
    else:
        nw, sc, sh = nxt
        in_specs += [pl.BlockSpec((1, d), lambda i: (0, 0)), vec, vec]
        args += [nw, sc, sh]
        out_specs = (pl.BlockSpec((tm, d), row), pl.BlockSpec((tm, d), row))
        out_shape = (jax.ShapeDtypeStruct((m, d), F32), jax.ShapeDtypeStruct((m, d), BF16))
    return pl.pallas_call(
        functools.partial(_outproj_kernel, last=last),
        grid=(m // tm,),
        in_specs=in_specs,
        out_specs=out_specs,
        out_shape=out_shape,
        compiler_params=_params(("parallel",)),
        name="outproj_last" if last else "outproj",
    )(*args)


def kernel(x, c, norm_w, ada_w, ada_b, w_in, ret_gn_w, attn_sinks, w_ret_o, w_swa_o, w_out, final_norm_w):
    b, s, d = x.shape
    depth = norm_w.shape[0]
    m = b * s
    ret_v_w = ret_gn_w.shape[1]
    ret_qk_w = ret_v_w // 2
    swa_q_w = d
    swa_kv_w = SWA_KV_HEADS * SWA_HEAD_DIM
    tn = 1024
    tm = 2048
    assert w_in.shape[2] == 2 * ret_qk_w + 2 * ret_v_w + 2 * swa_q_w + 2 * swa_kv_w + 2 * d
    assert ret_qk_w == tn and 2 * swa_kv_w == tn and ret_v_w == 2 * tn and d == 2 * tn and s % tm == 0

    c_t = jnp.pad(c.T, ((0, 0), (0, 8 - b)))
    mod = _modulation(c_t, ada_w, ada_b.reshape(depth, 1, 3 * d), b, tn=1536)
    shift = mod[:, :, :d].reshape(depth, b, 1, d)
    scale = mod[:, :, d:2 * d].reshape(depth, b, 1, d)
    gate = mod[:, :, 2 * d:].reshape(depth, b, 1, d)

    half = ret_qk_w // RET_HEADS // 2
    inv = 1.0 / (RET_ROT_BASE ** jnp.linspace(0.0, 1.0, half, dtype=F32))
    ang = jnp.arange(s, dtype=F32)[:, None] * inv[None, :]
    cos2 = jnp.concatenate([jnp.cos(ang), jnp.cos(ang)], axis=-1)
    sin2 = jnp.concatenate([-jnp.sin(ang), jnp.sin(ang)], axis=-1)
    k_scale = float(2 * half) ** -0.5
    cos_tab = jnp.stack([cos2, cos2 * k_scale])
    sin_tab = jnp.stack([sin2, sin2 * k_scale])

    gn_w3 = ret_gn_w.reshape(depth, 1, ret_v_w)
    nw2 = norm_w.reshape(depth, 1, d)

    h = x.reshape(m, d)
    u = _first_u(h, nw2[0], scale[0], shift[0], s, tm=512)

    gen_map = lambda j: jnp.where(j < 2, j + 2, jnp.where(j < 4, j + 4, jnp.where(j < 6, j,
                                  jnp.where(j < 12, j + 3, 8))))
    gen_tiles = 13
    out_weights = (w_ret_o, w_swa_o, w_out)
    gen_kernel = functools.partial(_inproj_gen_kernel, halves=2, silu_tiles=(4, 8), sigmoid_tiles=(8, 12),
                                   n_cast=len(out_weights))
    rot_kernel = functools.partial(_inproj_rot_kernel, halves=2)
    per_batch = s // tm
    hd2 = cos_tab.shape[-1]
    tab_spec = pl.BlockSpec((None, tm, hd2), lambda j, i: (j, i % per_batch, 0))

    for l in range(depth):
        (rqk,) = _inproj(rot_kernel, u, w_in, l, lambda j: j, 2, tm, tn, "inproj_rot",
                         extra_specs=(tab_spec, tab_spec), extra_args=(cos_tab, sin_tab))
        cast_in, cast_out, cast_shapes = _cast_slab_specs(out_weights, l, gen_tiles * (m // tm), m // tm)
        proj, wr_bf, ws_bf, wo_bf = _inproj(gen_kernel, u, w_in, l, gen_map, gen_tiles, tm, tn, "inproj_gen",
                                            extra_specs=cast_in, extra_args=out_weights,
                                            extra_out_specs=cast_out, extra_out_shapes=cast_shapes)
        r = _retention(rqk, proj, gn_w3, l, s, t=512, gate_blk=2)
        a = _swa(proj, attn_sinks[l], s, swa_q_w, swa_kv_w, q_blk=1, k_blk=24, v_blk=25, g_blk=3,
                 blocks_per_step=4)
        if l + 1 < depth:
            h, u = _outproj(r, a, proj, 2, h, gate[l], wr_bf, ws_bf, wo_bf, s, tm=256,
                            nxt=(nw2[l + 1], scale[l + 1], shift[l + 1]))
        else:
            h = _outproj(r, a, proj, 2, h, gate[l], wr_bf, ws_bf, wo_bf, s, tm=256,
                         final_w=final_norm_w.reshape(1, d))
    return h.reshape(b, s, d)
```

```python
import functools

import numpy as np
import jax
import jax.numpy as jnp
from jax import lax
from jax.experimental import pallas as pl
from jax.experimental.pallas import tpu as pltpu

F32 = jnp.float32
BF16 = jnp.bfloat16

RET_HEADS = 8
RET_ROT_BASE = 10000.0
SWA_HEAD_DIM = 64
SWA_KV_HEADS = 8
SWA_BLOCK = 128
EPS = 1e-6

RET_CHUNK = 256

V7X_VMEM_BYTES = 64 * 1024 * 1024
VMEM_LIMIT_BYTES = V7X_VMEM_BYTES - 6 * 1024 * 1024

NEG_BIG = -1e30


def _params(semantics):
    return pltpu.CompilerParams(dimension_semantics=semantics, vmem_limit_bytes=VMEM_LIMIT_BYTES)


def _sigmoid(y):
    return 0.5 * jnp.tanh(0.5 * y) + 0.5


def _mod_kernel(ct_ref, w_ref, b_ref, o_ref, *, batch):
    ct = ct_ref[...]
    ca = ct * _sigmoid(ct)
    w = w_ref[...]
    for b in range(batch):
        o_ref[b:b + 1, :] = jnp.sum(w * ca[:, b:b + 1], axis=0, keepdims=True) + b_ref[...]


def _modulation(c_t, ada_w, ada_b3, batch, tn):
    depth, d, n3 = ada_w.shape
    cols = c_t.shape[1]
    return pl.pallas_call(
        functools.partial(_mod_kernel, batch=batch),
        grid=(depth, n3 // tn),
        in_specs=[
            pl.BlockSpec((d, cols), lambda l, j: (0, 0)),
            pl.BlockSpec((None, d, tn), lambda l, j: (l, 0, j)),
            pl.BlockSpec((None, 1, tn), lambda l, j: (l, 0, j)),
        ],
        out_specs=pl.BlockSpec((None, batch, tn), lambda l, j: (l, 0, j)),
        out_shape=jax.ShapeDtypeStruct((depth, batch, n3), F32),
        compiler_params=_params(("parallel", "parallel")),
        name="adaln_modulation",
    )(c_t, ada_w, ada_b3)


def _modulated_norm(h, nw, scale, shift):
    ms = jnp.mean(h * h, axis=-1, keepdims=True)
    y = h * lax.rsqrt(ms + EPS) * nw
    return y * (1.0 + scale) + shift


def _u_kernel(x_ref, nw_ref, sc_ref, sh_ref, u_ref):
    u_ref[...] = _modulated_norm(x_ref[...], nw_ref[...], sc_ref[...], sh_ref[...]).astype(BF16)


def _first_u(x2, nw, scale, shift, seq, tm):
    m, d = x2.shape
    per_batch = seq // tm
    return pl.pallas_call(
        _u_kernel,
        grid=(m // tm,),
        in_specs=[
            pl.BlockSpec((tm, d), lambda i: (i, 0)),
            pl.BlockSpec((1, d), lambda i: (0, 0)),
            pl.BlockSpec((None, 1, d), lambda i: (i // per_batch, 0, 0)),
            pl.BlockSpec((None, 1, d), lambda i: (i // per_batch, 0, 0)),
        ],
        out_specs=pl.BlockSpec((tm, d), lambda i: (i, 0)),
        out_shape=jax.ShapeDtypeStruct((m, d), BF16),
        compiler_params=_params(("parallel",)),
        name="first_modulated_norm",
    )(x2, nw, scale, shift)


def _inproj_rot_kernel(u_ref, w_ref, cos_ref, sin_ref, o_ref, wbf_ref, *, halves):
    @pl.when(pl.program_id(1) == 0)
    def _():
        wbf_ref[...] = w_ref[...].astype(BF16)

    hd = cos_ref.shape[-1]
    sub = u_ref.shape[0] // halves
    for part in range(halves):
        rows = slice(part * sub, (part + 1) * sub)
        y = jnp.dot(u_ref[rows, :], wbf_ref[...], preferred_element_type=F32)
        cs = cos_ref[rows, :]
        sn = sin_ref[rows, :]
        for h in range(y.shape[1] // hd):
            yh = y[:, h * hd:(h + 1) * hd]
            o_ref[rows, h * hd:(h + 1) * hd] = (yh * cs + pltpu.roll(yh, hd // 2, 1) * sn).astype(BF16)


def _inproj_gen_kernel(*refs, halves, silu_tiles, sigmoid_tiles, n_cast):
    u_ref, w_ref = refs[:2]
    cast_in = refs[2:2 + n_cast]
    o_ref = refs[2 + n_cast]
    cast_out = refs[3 + n_cast:3 + 2 * n_cast]
    wbf_ref = refs[3 + 2 * n_cast]

    @pl.when(pl.program_id(1) == 0)
    def _():
        wbf_ref[...] = w_ref[...].astype(BF16)

    j = pl.program_id(0)
    is_silu = jnp.logical_and(j >= silu_tiles[0], j < silu_tiles[1])
    is_sig = jnp.logical_and(j >= sigmoid_tiles[0], j < sigmoid_tiles[1])
    sub = u_ref.shape[0] // halves

    def body(epilogue):
        for src, dst in zip(cast_in, cast_out):
            dst[...] = src[...].astype(BF16)
        for part in range(halves):
            rows = slice(part * sub, (part + 1) * sub)
            y = jnp.dot(u_ref[rows, :], wbf_ref[...], preferred_element_type=F32)
            o_ref[rows, :] = epilogue(y).astype(BF16)

    @pl.when(is_silu)
    def _():
        body(lambda y: y * _sigmoid(y))

    @pl.when(is_sig)
    def _():
        body(_sigmoid)

    @pl.when(jnp.logical_not(jnp.logical_or(is_silu, is_sig)))
    def _():
        body(lambda y: y)


def _inproj(kernel_fn, u, w_in, layer, col_map, n_tiles, tm, tn, name, extra_specs=(), extra_args=(),
            extra_out_specs=(), extra_out_shapes=()):
    m, d = u.shape
    return pl.pallas_call(
        kernel_fn,
        grid=(n_tiles, m // tm),
        in_specs=[
            pl.BlockSpec((tm, d), lambda j, i: (i, 0)),
            pl.BlockSpec((None, d, tn), lambda j, i: (layer, 0, col_map(j))),
        ] + list(extra_specs),
        out_specs=[pl.BlockSpec((tm, tn), lambda j, i: (i, j))] + list(extra_out_specs),
        out_shape=[jax.ShapeDtypeStruct((m, n_tiles * tn), BF16)] + list(extra_out_shapes),
        scratch_shapes=[pltpu.VMEM((d, tn), BF16)],
        compiler_params=_params(("arbitrary", "arbitrary")),
        name=name,
    )(u, w_in, *extra_args)


def _cast_slab_specs(weights, layer, n_steps, row_tiles):
    n_slabs = 1 << ((n_steps // len(weights)).bit_length() - 1)
    in_specs, out_specs, out_shapes = [], [], []
    for idx, w in enumerate(weights):
        rows, cols = w.shape[1:]
        slab = rows // n_slabs
        assert slab * n_slabs == rows and slab % 16 == 0

        def slab_of(j, i, idx=idx):
            return jnp.clip(j * row_tiles + i - idx * n_slabs, 0, n_slabs - 1)

        in_specs.append(pl.BlockSpec((None, slab, cols), lambda j, i, f=slab_of: (layer, f(j, i), 0)))
        out_specs.append(pl.BlockSpec((slab, cols), lambda j, i, f=slab_of: (f(j, i), 0)))
        out_shapes.append(jax.ShapeDtypeStruct((rows, cols), BF16))
    return in_specs, out_specs, out_shapes


def _retention_kernel(qk_ref, v_ref, g_ref, gnw_ref, o_ref, state_ref, *, heads, chunk, log_gammas):
    @pl.when(pl.program_id(1) == 0)
    def _():
        state_ref[...] = jnp.zeros_like(state_ref)

    t = qk_ref.shape[0]
    dk = qk_ref.shape[1] // (2 * heads)
    dv = v_ref.shape[1] // heads
    row = lax.broadcasted_iota(jnp.int32, (chunk, chunk), 0)
    col = lax.broadcasted_iota(jnp.int32, (chunk, chunk), 1)
    causal = row >= col
    pos = lax.broadcasted_iota(jnp.int32, (chunk, 1), 0).astype(F32)

    for h in range(heads):
        lg = log_gammas[h]
        q_scale = jnp.exp(lg * pos)
        k_scale = jnp.exp(-lg * pos)
        g_chunk = float(np.exp(lg * chunk))
        gnw = gnw_ref[:, h * dv:(h + 1) * dv]
        carry = state_ref[h]
        for c in range(t // chunk):
            rows = slice(c * chunk, (c + 1) * chunk)
            qa = (qk_ref[rows, h * dk:(h + 1) * dk].astype(F32) * q_scale).astype(BF16)
            kb = (qk_ref[rows, (heads + h) * dk:(heads + h + 1) * dk].astype(F32) * k_scale).astype(BF16)
            v = v_ref[rows, h * dv:(h + 1) * dv]
            s = lax.dot_general(qa, kb, (((1,), (1,)), ((), ())), preferred_element_type=F32)
            lhs = jnp.concatenate([jnp.where(causal, s, 0.0).astype(BF16), qa], axis=1)
            rhs = jnp.concatenate([v, carry.astype(BF16)], axis=0)
            o = jnp.dot(lhs, rhs, preferred_element_type=F32)
            kv = lax.dot_general(kb, v, (((0,), (0,)), ((), ())), preferred_element_type=F32)
            carry = (carry + kv) * g_chunk
            mu = jnp.mean(o, axis=-1, keepdims=True)
            dlt = o - mu
            var = jnp.mean(dlt * dlt, axis=-1, keepdims=True)
            rn = dlt * lax.rsqrt(var + EPS) * gnw
            o_ref[rows, h * dv:(h + 1) * dv] = (rn * g_ref[rows, h * dv:(h + 1) * dv].astype(F32)).astype(BF16)
        state_ref[h] = carry


def _retention(rqk, proj, gn_w3, layer, seq, t, gate_blk):
    m, qk_w = rqk.shape
    v_w = gn_w3.shape[-1]
    per_batch = seq // t
    log_gammas = tuple(float(np.log1p(-np.exp2(-5.0 - h))) for h in range(RET_HEADS))
    dk = qk_w // (2 * RET_HEADS)
    dv = v_w // RET_HEADS
    return pl.pallas_call(
        functools.partial(_retention_kernel, heads=RET_HEADS, chunk=RET_CHUNK, log_gammas=log_gammas),
        grid=(m // seq, per_batch),
        in_specs=[
            pl.BlockSpec((t, qk_w), lambda b, s: (b * per_batch + s, 0)),
            pl.BlockSpec((t, v_w), lambda b, s: (b * per_batch + s, 0)),
            pl.BlockSpec((t, v_w), lambda b, s: (b * per_batch + s, gate_blk)),
            pl.BlockSpec((None, 1, v_w), lambda b, s: (layer, 0, 0)),
        ],
        out_specs=pl.BlockSpec((t, v_w), lambda b, s: (b * per_batch + s, 0)),
        out_shape=jax.ShapeDtypeStruct((m, v_w), BF16),
        scratch_shapes=[pltpu.VMEM((RET_HEADS, dk, dv), F32)],
        compiler_params=_params(("parallel", "arbitrary")),
        name="retention",
    )(rqk, proj, proj, gn_w3)


def _swa_body(sink_ref, q_ref, kp_ref, kc_ref, vp_ref, vc_ref, g_ref, o_ref, *, kv_heads, group, hd, blk,
              first):
    lanes = 2 * hd
    log2e = float(np.log2(np.e))
    lane = lax.broadcasted_iota(jnp.int32, (1, lanes), 1)
    lo_f = lane < hd
    scale = float(hd) ** -0.5 * log2e
    qmask = (jnp.where(lo_f, scale, 0.0).astype(BF16), jnp.where(lo_f, 0.0, scale).astype(BF16))
    key = lax.broadcasted_iota(jnp.int32, (blk, 2 * blk), 0)
    qry = jnp.bitwise_and(lax.broadcasted_iota(jnp.int32, (blk, 2 * blk), 1), blk - 1)
    upper = key > qry
    head_lo = lax.broadcasted_iota(jnp.int32, (1, 2 * blk), 1) < blk

    def dup(tile_bf16, parity):
        tile = tile_bf16.astype(F32)
        rolled = pltpu.roll(tile, hd, 1)
        keep_lo = lo_f if parity == 0 else jnp.logical_not(lo_f)
        return jnp.where(keep_lo, tile, rolled)

    def key_value_operands(k_tiles, v_tiles):
        ks = [dup(k_tiles[:, (h // 2) * lanes:(h // 2 + 1) * lanes], h % 2).astype(BF16)
              for h in range(kv_heads)]
        vts = [v_tiles[:, t * lanes:(t + 1) * lanes].astype(F32).T.astype(BF16)
               for t in range(kv_heads // 2)]
        return ks, vts

    nt = (((1,), (1,)), ((), ()))
    prev_ops = None if first else key_value_operands(kp_ref[...], vp_ref[...])
    for bi in range(q_ref.shape[0] // blk):
        rows = slice(bi * blk, (bi + 1) * blk)
        cur_ops = key_value_operands(kc_ref[rows, :], vc_ref[rows, :])
        no_prev = prev_ops is None
        for h in range(kv_heads):
            hrows = slice((h % 2) * hd, (h % 2 + 1) * hd)
            k_cur = cur_ops[0][h]
            vt_cur = cur_ops[1][h // 2]
            for gp in range(group // 2):
                qt = (h * group) // 2 + gp
                qcol = slice(qt * lanes, (qt + 1) * lanes)
                qtile = q_ref[rows, qcol]
                qq = jnp.concatenate([qtile * qmask[0], qtile * qmask[1]], axis=0)
                s_cur = lax.dot_general(k_cur, qq, nt, preferred_element_type=F32)
                if no_prev:
                    s = jnp.where(upper, NEG_BIG, s_cur)
                else:
                    s = jnp.where(upper, lax.dot_general(prev_ops[0][h], qq, nt, preferred_element_type=F32),
                                  s_cur)
                hq = h * group + 2 * gp
                sink = jnp.where(head_lo, sink_ref[hq], sink_ref[hq + 1]) * log2e
                mx = jnp.maximum(jnp.max(s, axis=0, keepdims=True), sink)
                p = jnp.exp2(s - mx)
                inv = 1.0 / (jnp.sum(p, axis=0, keepdims=True) + jnp.exp2(sink - mx))
                if no_prev:
                    ot = jnp.dot(vt_cur, p.astype(BF16), preferred_element_type=F32)
                else:
                    ot = (jnp.dot(prev_ops[1][h // 2], jnp.where(upper, p, 0.0).astype(BF16),
                                  preferred_element_type=F32)
                          + jnp.dot(vt_cur, jnp.where(upper, 0.0, p).astype(BF16), preferred_element_type=F32))
                ot = ot[hrows] * inv
                pair_t = jnp.concatenate([ot[:, :blk], ot[:, blk:]], axis=0)
                o_ref[rows, qcol] = (pair_t.T * g_ref[rows, qcol].astype(F32)).astype(BF16)
        prev_ops = cur_ops


def _swa_kernel(*refs, **kw):
    is_first = pl.program_id(1) == 0

    @pl.when(is_first)
    def _():
        _swa_body(*refs, first=True, **kw)

    @pl.when(jnp.logical_not(is_first))
    def _():
        _swa_body(*refs, first=False, **kw)


def _swa(proj, sinks_l, seq, q_w, kv_w, q_blk, k_blk, v_blk, g_blk, blocks_per_step):
    m = proj.shape[0]
    t = blocks_per_step * SWA_BLOCK
    steps = seq // t
    kv_heads = SWA_KV_HEADS
    group = q_w // SWA_HEAD_DIM // kv_heads
    cur = lambda b, n: b * steps + n
    prev = lambda b, n: (b * steps + n) * blocks_per_step - jnp.minimum(n, 1)
    return pl.pallas_call(
        functools.partial(_swa_kernel, kv_heads=kv_heads, group=group, hd=SWA_HEAD_DIM, blk=SWA_BLOCK),
        grid=(m // seq, steps),
        in_specs=[
            pl.BlockSpec(memory_space=pltpu.SMEM),
            pl.BlockSpec((t, q_w), lambda b, n: (cur(b, n), q_blk)),
            pl.BlockSpec((SWA_BLOCK, kv_w), lambda b, n: (prev(b, n), k_blk)),
            pl.BlockSpec((t, kv_w), lambda b, n: (cur(b, n), k_blk)),
            pl.BlockSpec((SWA_BLOCK, kv_w), lambda b, n: (prev(b, n), v_blk)),
            pl.BlockSpec((t, kv_w), lambda b, n: (cur(b, n), v_blk)),
            pl.BlockSpec((t, q_w), lambda b, n: (cur(b, n), g_blk)),
        ],
        out_specs=pl.BlockSpec((t, q_w), lambda b, n: (cur(b, n), 0)),
        out_shape=jax.ShapeDtypeStruct((m, q_w), BF16),
        compiler_params=_params(("parallel", "arbitrary")),
        name="swa",
    )(sinks_l, proj, proj, proj, proj, proj, proj)


def _outproj_kernel(*refs, last):
    if last:
        r_ref, a_ref, sg_ref, h_ref, gate_ref, wr_ref, ws_ref, wo_ref, fw_ref, out_ref = refs
    else:
        (r_ref, a_ref, sg_ref, h_ref, gate_ref, wr_ref, ws_ref, wo_ref,
         nw_ref, sc_ref, sh_ref, hn_ref, u_ref) = refs
    d = h_ref.shape[1]
    y1 = jnp.dot(r_ref[...], wr_ref[...], preferred_element_type=F32)
    y2 = jnp.dot(a_ref[...], ws_ref[...], preferred_element_type=F32)
    merged = sg_ref[:, :d].astype(F32) * y1 + sg_ref[:, d:].astype(F32) * y2
    z = jnp.dot(merged.astype(BF16), wo_ref[...], preferred_element_type=F32)
    hn = h_ref[...] + gate_ref[...] * z
    if last:
        ms = jnp.mean(hn * hn, axis=-1, keepdims=True)
        out_ref[...] = hn * lax.rsqrt(ms + EPS) * fw_ref[...]
    else:
        hn_ref[...] = hn
        u_ref[...] = _modulated_norm(hn, nw_ref[...], sc_ref[...], sh_ref[...]).astype(BF16)


def _outproj(r, a, proj, sig_blk, h, gate, wr, ws, wo, seq, tm, nxt=None, final_w=None):
    m, d = h.shape
    per_batch = seq // tm
    last = nxt is None
    row = lambda i: (i, 0)
    vec = pl.BlockSpec((None, 1, d), lambda i: (i // per_batch, 0, 0))
    wspec = lambda rows: pl.BlockSpec((rows, d), lambda i: (0, 0), pipeline_mode=pl.Buffered(1))
    in_specs = [
        pl.BlockSpec((tm, r.shape[1]), row), pl.BlockSpec((tm, d), row),
        pl.BlockSpec((tm, 2 * d), lambda i: (i, sig_blk)),
        pl.BlockSpec((tm, d), row), vec, wspec(r.shape[1]), wspec(d), wspec(d),
    ]
    args = [r, a, proj, h, gate, wr, ws, wo]
    if last:
        in_specs.append(pl.BlockSpec((1, d), lambda i: (0, 0)))
        args.append(final_w)
        out_specs = pl.BlockSpec((tm, d), row)
        out_shape = jax.ShapeDtypeStruct((m, d), F32)
    else:
        nw, sc, sh = nxt
        in_specs += [pl.BlockSpec((1, d), lambda i: (0, 0)), vec, vec]
        args += [nw, sc, sh]
        out_specs = (pl.BlockSpec((tm, d), row), pl.BlockSpec((tm, d), row))
        out_shape = (jax.ShapeDtypeStruct((m, d), F32), jax.ShapeDtypeStruct((m, d), BF16))
    return pl.pallas_call(
        functools.partial(_outproj_kernel, last=last),
        grid=(m // tm,),
        in_specs=in_specs,
        out_specs=out_specs,
        out_shape=out_shape,
        compiler_params=_params(("parallel",)),
        name="outproj_last" if last else "outproj",
    )(*args)


def kernel(x, c, norm_w, ada_w, ada_b, w_in, ret_gn_w, attn_sinks, w_ret_o, w_swa_o, w_out, final_norm_w):
    b, s, d = x.shape
    depth = norm_w.shape[0]
    m = b * s
    ret_v_w = ret_gn_w.shape[1]
    ret_qk_w = ret_v_w // 2
    swa_q_w = d
    swa_kv_w = SWA_KV_HEADS * SWA_HEAD_DIM
    tn = 1024
    tm = 2048
    assert w_in.shape[2] == 2 * ret_qk_w + 2 * ret_v_w + 2 * swa_q_w + 2 * swa_kv_w + 2 * d
    assert ret_qk_w == tn and 2 * swa_kv_w == tn and ret_v_w == 2 * tn and d == 2 * tn and s % tm == 0

    c_t = jnp.pad(c.T, ((0, 0), (0, 8 - b)))
    mod = _modulation(c_t, ada_w, ada_b.reshape(depth, 1, 3 * d), b, tn=1536)
    shift = mod[:, :, :d].reshape(depth, b, 1, d)
    scale = mod[:, :, d:2 * d].reshape(depth, b, 1, d)
    gate = mod[:, :, 2 * d:].reshape(depth, b, 1, d)

    half = ret_qk_w // RET_HEADS // 2
    inv = (1.0 / (RET_ROT_BASE ** np.linspace(0.0, 1.0, half, dtype=np.float32))).astype(np.float32)
    ang = np.arange(s, dtype=np.float32)[:, None] * inv[None, :]
    cos2 = np.concatenate([np.cos(ang), np.cos(ang)], axis=-1)
    sin2 = np.concatenate([-np.sin(ang), np.sin(ang)], axis=-1)
    k_scale = np.float32(float(2 * half) ** -0.5)
    cos_tab = jnp.asarray(np.stack([cos2, cos2 * k_scale]), dtype=F32)
    sin_tab = jnp.asarray(np.stack([sin2, sin2 * k_scale]), dtype=F32)

    gn_w3 = ret_gn_w.reshape(depth, 1, ret_v_w)
    nw2 = norm_w.reshape(depth, 1, d)

    h = x.reshape(m, d)
    u = _first_u(h, nw2[0], scale[0], shift[0], s, tm=1024)

    gen_map = lambda j: jnp.where(j < 2, j + 2, jnp.where(j < 4, j + 4, jnp.where(j < 6, j,
                                  jnp.where(j < 12, j + 3, 8))))
    gen_tiles = 13
    out_weights = (w_ret_o, w_swa_o, w_out)
    gen_kernel = functools.partial(_inproj_gen_kernel, halves=2, silu_tiles=(4, 8), sigmoid_tiles=(8, 12),
                                   n_cast=len(out_weights))
    rot_kernel = functools.partial(_inproj_rot_kernel, halves=2)
    per_batch = s // tm
    hd2 = cos_tab.shape[-1]
    tab_spec = pl.BlockSpec((None, tm, hd2), lambda j, i: (j, i % per_batch, 0))

    for l in range(depth):
        (rqk,) = _inproj(rot_kernel, u, w_in, l, lambda j: j, 2, tm, tn, "inproj_rot",
                         extra_specs=(tab_spec, tab_spec), extra_args=(cos_tab, sin_tab))
        cast_in, cast_out, cast_shapes = _cast_slab_specs(out_weights, l, gen_tiles * (m // tm), m // tm)
        proj, wr_bf, ws_bf, wo_bf = _inproj(gen_kernel, u, w_in, l, gen_map, gen_tiles, tm, tn, "inproj_gen",
                                            extra_specs=cast_in, extra_args=out_weights,
                                            extra_out_specs=cast_out, extra_out_shapes=cast_shapes)
        r = _retention(rqk, proj, gn_w3, l, s, t=1024, gate_blk=2)
        a = _swa(proj, attn_sinks[l], s, swa_q_w, swa_kv_w, q_blk=1, k_blk=24, v_blk=25, g_blk=3,
                 blocks_per_step=8)
        if l + 1 < depth:
            h, u = _outproj(r, a, proj, 2, h, gate[l], wr_bf, ws_bf, wo_bf, s, tm=256,
                            nxt=(nw2[l + 1], scale[l + 1], shift[l + 1]))
        else:
            h = _outproj(r, a, proj, 2, h, gate[l], wr_bf, ws_bf, wo_bf, s, tm=256,
                         final_w=final_norm_w.reshape(1, d))
    return h.reshape(b, s, d)
```

```python
import functools

import numpy as np
import jax
import jax.numpy as jnp
from jax import lax
from jax.experimental import pallas as pl
from jax.experimental.pallas import tpu as pltpu

F32 = jnp.float32
BF16 = jnp.bfloat16

RET_HEADS = 8
RET_ROT_BASE = 10000.0
SWA_HEAD_DIM = 64
SWA_KV_HEADS = 8
SWA_BLOCK = 128
EPS = 1e-6

RET_CHUNK = 256

V7X_VMEM_BYTES = 64 * 1024 * 1024
VMEM_LIMIT_BYTES = V7X_VMEM_BYTES - 6 * 1024 * 1024

NEG_BIG = -1e30


def _params(semantics):
    return pltpu.CompilerParams(dimension_semantics=semantics, vmem_limit_bytes=VMEM_LIMIT_BYTES)


def _sigmoid(y):
    return 0.5 * jnp.tanh(0.5 * y) + 0.5


def _mod_kernel(ct_ref, w_ref, b_ref, o_ref, *, batch):
    ct = ct_ref[...]
    ca = ct * _sigmoid(ct)
    w = w_ref[...]
    for b in range(batch):
        o_ref[b:b + 1, :] = jnp.sum(w * ca[:, b:b + 1], axis=0, keepdims=True) + b_ref[...]


def _modulation(c_t, ada_w, ada_b3, batch, tn):
    depth, d, n3 = ada_w.shape
    cols = c_t.shape[1]
    return pl.pallas_call(
        functools.partial(_mod_kernel, batch=batch),
        grid=(depth, n3 // tn),
        in_specs=[
            pl.BlockSpec((d, cols), lambda l, j: (0, 0)),
            pl.BlockSpec((None, d, tn), lambda l, j: (l, 0, j)),
            pl.BlockSpec((None, 1, tn), lambda l, j: (l, 0, j)),
        ],
        out_specs=pl.BlockSpec((None, batch, tn), lambda l, j: (l, 0, j)),
        out_shape=jax.ShapeDtypeStruct((depth, batch, n3), F32),
        compiler_params=_params(("parallel", "parallel")),
        name="adaln_modulation",
    )(c_t, ada_w, ada_b3)


def _modulated_norm(h, nw, scale, shift):
    ms = jnp.mean(h * h, axis=-1, keepdims=True)
    y = h * lax.rsqrt(ms + EPS) * nw
    return y * (1.0 + scale) + shift


def _u_kernel(x_ref, nw_ref, sc_ref, sh_ref, u_ref):
    u_ref[...] = _modulated_norm(x_ref[...], nw_ref[...], sc_ref[...], sh_ref[...]).astype(BF16)


def _first_u(x2, nw, scale, shift, seq, tm):
    m, d = x2.shape
    per_batch = seq // tm
    return pl.pallas_call(
        _u_kernel,
        grid=(m // tm,),
        in_specs=[
            pl.BlockSpec((tm, d), lambda i: (i, 0)),
            pl.BlockSpec((1, d), lambda i: (0, 0)),
            pl.BlockSpec((None, 1, d), lambda i: (i // per_batch, 0, 0)),
            pl.BlockSpec((None, 1, d), lambda i: (i // per_batch, 0, 0)),
        ],
        out_specs=pl.BlockSpec((tm, d), lambda i: (i, 0)),
        out_shape=jax.ShapeDtypeStruct((m, d), BF16),
        compiler_params=_params(("parallel",)),
        name="first_modulated_norm",
    )(x2, nw, scale, shift)


def _inproj_rot_kernel(u_ref, w_ref, cos_ref, sin_ref, o_ref, wbf_ref, *, halves):
    @pl.when(pl.program_id(1) == 0)
    def _():
        wbf_ref[...] = w_ref[...].astype(BF16)

    hd = cos_ref.shape[-1]
    sub = u_ref.shape[0] // halves
    for part in range(halves):
        rows = slice(part * sub, (part + 1) * sub)
        y = jnp.dot(u_ref[rows, :], wbf_ref[...], preferred_element_type=F32)
        cs = cos_ref[rows, :]
        sn = sin_ref[rows, :]
        for h in range(y.shape[1] // hd):
            yh = y[:, h * hd:(h + 1) * hd]
            o_ref[rows, h * hd:(h + 1) * hd] = (yh * cs + pltpu.roll(yh, hd // 2, 1) * sn).astype(BF16)


def _inproj_gen_kernel(*refs, halves, silu_tiles, sigmoid_tiles, n_cast):
    u_ref, w_ref = refs[:2]
    cast_in = refs[2:2 + n_cast]
    o_ref = refs[2 + n_cast]
    cast_out = refs[3 + n_cast:3 + 2 * n_cast]
    wbf_ref = refs[3 + 2 * n_cast]

    @pl.when(pl.program_id(1) == 0)
    def _():
        wbf_ref[...] = w_ref[...].astype(BF16)

    j = pl.program_id(0)
    is_silu = jnp.logical_and(j >= silu_tiles[0], j < silu_tiles[1])
    is_sig = jnp.logical_and(j >= sigmoid_tiles[0], j < sigmoid_tiles[1])
    sub = u_ref.shape[0] // halves

    def body(epilogue):
        for src, dst in zip(cast_in, cast_out):
            dst[...] = src[...].astype(BF16)
        for part in range(halves):
            rows = slice(part * sub, (part + 1) * sub)
            y = jnp.dot(u_ref[rows, :], wbf_ref[...], preferred_element_type=F32)
            o_ref[rows, :] = epilogue(y).astype(BF16)

    @pl.when(is_silu)
    def _():
        body(lambda y: y * _sigmoid(y))

    @pl.when(is_sig)
    def _():
        body(_sigmoid)

    @pl.when(jnp.logical_not(jnp.logical_or(is_silu, is_sig)))
    def _():
        body(lambda y: y)


def _inproj(kernel_fn, u, w_in, layer, col_map, n_tiles, tm, tn, name, extra_specs=(), extra_args=(),
            extra_out_specs=(), extra_out_shapes=()):
    m, d = u.shape
    return pl.pallas_call(
        kernel_fn,
        grid=(n_tiles, m // tm),
        in_specs=[
            pl.BlockSpec((tm, d), lambda j, i: (i, 0)),
            pl.BlockSpec((None, d, tn), lambda j, i: (layer, 0, col_map(j))),
        ] + list(extra_specs),
        out_specs=[pl.BlockSpec((tm, tn), lambda j, i: (i, j))] + list(extra_out_specs),
        out_shape=[jax.ShapeDtypeStruct((m, n_tiles * tn), BF16)] + list(extra_out_shapes),
        scratch_shapes=[pltpu.VMEM((d, tn), BF16)],
        compiler_params=_params(("arbitrary", "arbitrary")),
        name=name,
    )(u, w_in, *extra_args)


def _cast_slab_specs(weights, layer, n_steps, row_tiles):
    n_slabs = 1 << ((n_steps // len(weights)).bit_length() - 1)
    in_specs, out_specs, out_shapes = [], [], []
    for idx, w in enumerate(weights):
        rows, cols = w.shape[1:]
        slab = rows // n_slabs
        assert slab * n_slabs == rows and slab % 16 == 0

        def slab_of(j, i, idx=idx):
            return jnp.clip(j * row_tiles + i - idx * n_slabs, 0, n_slabs - 1)

        in_specs.append(pl.BlockSpec((None, slab, cols), lambda j, i, f=slab_of: (layer, f(j, i), 0)))
        out_specs.append(pl.BlockSpec((slab, cols), lambda j, i, f=slab_of: (f(j, i), 0)))
        out_shapes.append(jax.ShapeDtypeStruct((rows, cols), BF16))
    return in_specs, out_specs, out_shapes


def _retention_kernel(qk_ref, v_ref, gnw_ref, o_ref, state_ref, *, heads, chunk, log_gammas):
    @pl.when(pl.program_id(1) == 0)
    def _():
        state_ref[...] = jnp.zeros_like(state_ref)

    t = qk_ref.shape[0]
    dk = qk_ref.shape[1] // (2 * heads)
    dv = v_ref.shape[1] // heads
    row = lax.broadcasted_iota(jnp.int32, (chunk, chunk), 0)
    col = lax.broadcasted_iota(jnp.int32, (chunk, chunk), 1)
    causal = row >= col
    pos = lax.broadcasted_iota(jnp.int32, (chunk, 1), 0).astype(F32)

    for h in range(heads):
        lg = log_gammas[h]
        q_scale = jnp.exp(lg * pos)
        k_scale = jnp.exp(-lg * pos)
        g_chunk = float(np.exp(lg * chunk))
        gnw = gnw_ref[:, h * dv:(h + 1) * dv]
        carry = state_ref[h]
        for c in range(t // chunk):
            rows = slice(c * chunk, (c + 1) * chunk)
            qa = (qk_ref[rows, h * dk:(h + 1) * dk].astype(F32) * q_scale).astype(BF16)
            kb = (qk_ref[rows, (heads + h) * dk:(heads + h + 1) * dk].astype(F32) * k_scale).astype(BF16)
            v = v_ref[rows, h * dv:(h + 1) * dv]
            s = lax.dot_general(qa, kb, (((1,), (1,)), ((), ())), preferred_element_type=F32)
            lhs = jnp.concatenate([jnp.where(causal, s, 0.0).astype(BF16), qa], axis=1)
            rhs = jnp.concatenate([v, carry.astype(BF16)], axis=0)
            o = jnp.dot(lhs, rhs, preferred_element_type=F32)
            kv = lax.dot_general(kb, v, (((0,), (0,)), ((), ())), preferred_element_type=F32)
            carry = (carry + kv) * g_chunk
            mu = jnp.mean(o, axis=-1, keepdims=True)
            dlt = o - mu
            var = jnp.mean(dlt * dlt, axis=-1, keepdims=True)
            o_ref[rows, h * dv:(h + 1) * dv] = (dlt * lax.rsqrt(var + EPS) * gnw).astype(BF16)
        state_ref[h] = carry


def _retention(rqk, proj, gn_w3, layer, seq, t):
    m, qk_w = rqk.shape
    v_w = gn_w3.shape[-1]
    per_batch = seq // t
    log_gammas = tuple(float(np.log1p(-np.exp2(-5.0 - h))) for h in range(RET_HEADS))
    dk = qk_w // (2 * RET_HEADS)
    dv = v_w // RET_HEADS
    return pl.pallas_call(
        functools.partial(_retention_kernel, heads=RET_HEADS, chunk=RET_CHUNK, log_gammas=log_gammas),
        grid=(m // seq, per_batch),
        in_specs=[
            pl.BlockSpec((t, qk_w), lambda b, s: (b * per_batch + s, 0)),
            pl.BlockSpec((t, v_w), lambda b, s: (b * per_batch + s, 0)),
            pl.BlockSpec((None, 1, v_w), lambda b, s: (layer, 0, 0)),
        ],
        out_specs=pl.BlockSpec((t, v_w), lambda b, s: (b * per_batch + s, 0)),
        out_shape=jax.ShapeDtypeStruct((m, v_w), BF16),
        scratch_shapes=[pltpu.VMEM((RET_HEADS, dk, dv), F32)],
        compiler_params=_params(("parallel", "arbitrary")),
        name="retention",
    )(rqk, proj, gn_w3)


def _swa_body(sink_ref, q_ref, kp_ref, kc_ref, vp_ref, vc_ref, g_ref, o_ref, *, kv_heads, group, hd, blk,
              first):
    lanes = 2 * hd
    log2e = float(np.log2(np.e))
    lane = lax.broadcasted_iota(jnp.int32, (1, lanes), 1)
    lo_f = lane < hd
    scale = float(hd) ** -0.5 * log2e
    qmask = (jnp.where(lo_f, scale, 0.0).astype(BF16), jnp.where(lo_f, 0.0, scale).astype(BF16))
    key = lax.broadcasted_iota(jnp.int32, (blk, 2 * blk), 0)
    qry = jnp.bitwise_and(lax.broadcasted_iota(jnp.int32, (blk, 2 * blk), 1), blk - 1)
    upper = key > qry
    head_lo = lax.broadcasted_iota(jnp.int32, (1, 2 * blk), 1) < blk

    def dup(tile_bf16, parity):
        tile = tile_bf16.astype(F32)
        rolled = pltpu.roll(tile, hd, 1)
        keep_lo = lo_f if parity == 0 else jnp.logical_not(lo_f)
        return jnp.where(keep_lo, tile, rolled)

    def key_value_operands(k_tiles, v_tiles):
        ks = [dup(k_tiles[:, (h // 2) * lanes:(h // 2 + 1) * lanes], h % 2).astype(BF16)
              for h in range(kv_heads)]
        vts = [v_tiles[:, t * lanes:(t + 1) * lanes].astype(F32).T.astype(BF16)
               for t in range(kv_heads // 2)]
        return ks, vts

    nt = (((1,), (1,)), ((), ()))
    prev_ops = None if first else key_value_operands(kp_ref[...], vp_ref[...])
    for bi in range(q_ref.shape[0] // blk):
        rows = slice(bi * blk, (bi + 1) * blk)
        cur_ops = key_value_operands(kc_ref[rows, :], vc_ref[rows, :])
        no_prev = prev_ops is None
        for h in range(kv_heads):
            hrows = slice((h % 2) * hd, (h % 2 + 1) * hd)
            k_cur = cur_ops[0][h]
            vt_cur = cur_ops[1][h // 2]
            for gp in range(group // 2):
                qt = (h * group) // 2 + gp
                qcol = slice(qt * lanes, (qt + 1) * lanes)
                qtile = q_ref[rows, qcol]
                qq = jnp.concatenate([qtile * qmask[0], qtile * qmask[1]], axis=0)
                s_cur = lax.dot_general(k_cur, qq, nt, preferred_element_type=F32)
                if no_prev:
                    s = jnp.where(upper, NEG_BIG, s_cur)
                else:
                    s = jnp.where(upper, lax.dot_general(prev_ops[0][h], qq, nt, preferred_element_type=F32),
                                  s_cur)
                hq = h * group + 2 * gp
                sink = jnp.where(head_lo, sink_ref[hq], sink_ref[hq + 1]) * log2e
                mx = jnp.maximum(jnp.max(s, axis=0, keepdims=True), sink)
                p = jnp.exp2(s - mx)
                inv = 1.0 / (jnp.sum(p, axis=0, keepdims=True) + jnp.exp2(sink - mx))
                if no_prev:
                    ot = jnp.dot(vt_cur, p.astype(BF16), preferred_element_type=F32)
                else:
                    ot = (jnp.dot(prev_ops[1][h // 2], jnp.where(upper, p, 0.0).astype(BF16),
                                  preferred_element_type=F32)
                          + jnp.dot(vt_cur, jnp.where(upper, 0.0, p).astype(BF16), preferred_element_type=F32))
                ot = ot[hrows] * inv
                pair_t = jnp.concatenate([ot[:, :blk], ot[:, blk:]], axis=0)
                o_ref[rows, qcol] = (pair_t.T * g_ref[rows, qcol].astype(F32)).astype(BF16)
        prev_ops = cur_ops


def _swa_kernel(*refs, **kw):
    is_first = pl.program_id(1) == 0

    @pl.when(is_first)
    def _():
        _swa_body(*refs, first=True, **kw)

    @pl.when(jnp.logical_not(is_first))
    def _():
        _swa_body(*refs, first=False, **kw)


def _swa(proj, sinks_l, seq, q_w, kv_w, q_blk, k_blk, v_blk, g_blk, blocks_per_step):
    m = proj.shape[0]
    t = blocks_per_step * SWA_BLOCK
    steps = seq // t
    kv_heads = SWA_KV_HEADS
    group = q_w // SWA_HEAD_DIM // kv_heads
    cur = lambda b, n: b * steps + n
    prev = lambda b, n: (b * steps + n) * blocks_per_step - jnp.minimum(n, 1)
    return pl.pallas_call(
        functools.partial(_swa_kernel, kv_heads=kv_heads, group=group, hd=SWA_HEAD_DIM, blk=SWA_BLOCK),
        grid=(m // seq, steps),
        in_specs=[
            pl.BlockSpec(memory_space=pltpu.SMEM),
            pl.BlockSpec((t, q_w), lambda b, n: (cur(b, n), q_blk)),
            pl.BlockSpec((SWA_BLOCK, kv_w), lambda b, n: (prev(b, n), k_blk)),
            pl.BlockSpec((t, kv_w), lambda b, n: (cur(b, n), k_blk)),
            pl.BlockSpec((SWA_BLOCK, kv_w), lambda b, n: (prev(b, n), v_blk)),
            pl.BlockSpec((t, kv_w), lambda b, n: (cur(b, n), v_blk)),
            pl.BlockSpec((t, q_w), lambda b, n: (cur(b, n), g_blk)),
        ],
        out_specs=pl.BlockSpec((t, q_w), lambda b, n: (cur(b, n), 0)),
        out_shape=jax.ShapeDtypeStruct((m, q_w), BF16),
        compiler_params=_params(("parallel", "arbitrary")),
        name="swa",
    )(sinks_l, proj, proj, proj, proj, proj, proj)


def _outproj_kernel(*refs, last):
    if last:
        r_ref, rg_ref, a_ref, sg_ref, h_ref, gate_ref, wr_ref, ws_ref, wo_ref, fw_ref, out_ref = refs
    else:
        (r_ref, rg_ref, a_ref, sg_ref, h_ref, gate_ref, wr_ref, ws_ref, wo_ref,
         nw_ref, sc_ref, sh_ref, hn_ref, u_ref) = refs
    d = h_ref.shape[1]
    y1 = jnp.dot(r_ref[...] * rg_ref[...], wr_ref[...], preferred_element_type=F32)
    y2 = jnp.dot(a_ref[...], ws_ref[...], preferred_element_type=F32)
    merged = sg_ref[:, :d].astype(F32) * y1 + sg_ref[:, d:].astype(F32) * y2
    z = jnp.dot(merged.astype(BF16), wo_ref[...], preferred_element_type=F32)
    hn = h_ref[...] + gate_ref[...] * z
    if last:
        ms = jnp.mean(hn * hn, axis=-1, keepdims=True)
        out_ref[...] = hn * lax.rsqrt(ms + EPS) * fw_ref[...]
    else:
        hn_ref[...] = hn
        u_ref[...] = _modulated_norm(hn, nw_ref[...], sc_ref[...], sh_ref[...]).astype(BF16)


def _outproj(r, a, proj, rg_blk, sig_blk, h, gate, wr, ws, wo, seq, tm, nxt=None, final_w=None):
    m, d = h.shape
    per_batch = seq // tm
    last = nxt is None
    row = lambda i: (i, 0)
    vec = pl.BlockSpec((None, 1, d), lambda i: (i // per_batch, 0, 0))
    wspec = lambda rows: pl.BlockSpec((rows, d), lambda i: (0, 0), pipeline_mode=pl.Buffered(1))
    in_specs = [
        pl.BlockSpec((tm, r.shape[1]), row), pl.BlockSpec((tm, r.shape[1]), lambda i: (i, rg_blk)),
        pl.BlockSpec((tm, d), row), pl.BlockSpec((tm, 2 * d), lambda i: (i, sig_blk)),
        pl.BlockSpec((tm, d), row), vec, wspec(r.shape[1]), wspec(d), wspec(d),
    ]
    args = [r, proj, a, proj, h, gate, wr, ws, wo]
    if last:
        in_specs.append(pl.BlockSpec((1, d), lambda i: (0, 0)))
        args.append(final_w)
        out_specs = pl.BlockSpec((tm, d), row)
        out_shape = jax.ShapeDtypeStruct((m, d), F32)
    else:
        nw, sc, sh = nxt
        in_specs += [pl.BlockSpec((1, d), lambda i: (0, 0)), vec, vec]
        args += [nw, sc, sh]
        out_specs = (pl.BlockSpec((tm, d), row), pl.BlockSpec((tm, d), row))
        out_shape = (jax.ShapeDtypeStruct((m, d), F32), jax.ShapeDtypeStruct((m, d), BF16))
    return pl.pallas_call(
        functools.partial(_outproj_kernel, last=last),
        grid=(m // tm,),
        in_specs=in_specs,
        out_specs=out_specs,
        out_shape=out_shape,
        compiler_params=_params(("parallel",)),
        name="outproj_last" if last else "outproj",
    )(*args)


def kernel(x, c, norm_w, ada_w, ada_b, w_in, ret_gn_w, attn_sinks, w_ret_o, w_swa_o, w_out, final_norm_w):
    b, s, d = x.shape
    depth = norm_w.shape[0]
    m = b * s
    ret_v_w = ret_gn_w.shape[1]
    ret_qk_w = ret_v_w // 2
    swa_q_w = d
    swa_kv_w = SWA_KV_HEADS * SWA_HEAD_DIM
    tn = 1024
    tm = 2048
    assert w_in.shape[2] == 2 * ret_qk_w + 2 * ret_v_w + 2 * swa_q_w + 2 * swa_kv_w + 2 * d
    assert ret_qk_w == tn and 2 * swa_kv_w == tn and ret_v_w == 2 * tn and d == 2 * tn and s % tm == 0

    c_t = jnp.pad(c.T, ((0, 0), (0, 8 - b)))
    mod = _modulation(c_t, ada_w, ada_b.reshape(depth, 1, 3 * d), b, tn=1536)
    shift = mod[:, :, :d].reshape(depth, b, 1, d)
    scale = mod[:, :, d:2 * d].reshape(depth, b, 1, d)
    gate = mod[:, :, 2 * d:].reshape(depth, b, 1, d)

    half = ret_qk_w // RET_HEADS // 2
    inv = (1.0 / (RET_ROT_BASE ** np.linspace(0.0, 1.0, half, dtype=np.float32))).astype(np.float32)
    ang = np.arange(s, dtype=np.float32)[:, None] * inv[None, :]
    cos2 = np.concatenate([np.cos(ang), np.cos(ang)], axis=-1)
    sin2 = np.concatenate([-np.sin(ang), np.sin(ang)], axis=-1)
    k_scale = np.float32(float(2 * half) ** -0.5)
    cos_tab = jnp.asarray(np.stack([cos2, cos2 * k_scale]), dtype=F32)
    sin_tab = jnp.asarray(np.stack([sin2, sin2 * k_scale]), dtype=F32)

    gn_w3 = ret_gn_w.reshape(depth, 1, ret_v_w)
    nw2 = norm_w.reshape(depth, 1, d)

    h = x.reshape(m, d)
    u = _first_u(h, nw2[0], scale[0], shift[0], s, tm=1024)

    gen_map = lambda j: jnp.where(j < 2, j + 2, jnp.where(j < 4, j + 4, jnp.where(j < 6, j,
                                  jnp.where(j < 12, j + 3, 8))))
    gen_tiles = 13
    out_weights = (w_ret_o, w_swa_o, w_out)
    gen_kernel = functools.partial(_inproj_gen_kernel, halves=2, silu_tiles=(4, 8), sigmoid_tiles=(8, 12),
                                   n_cast=len(out_weights))
    rot_kernel = functools.partial(_inproj_rot_kernel, halves=2)
    per_batch = s // tm
    hd2 = cos_tab.shape[-1]
    tab_spec = pl.BlockSpec((None, tm, hd2), lambda j, i: (j, i % per_batch, 0))

    for l in range(depth):
        (rqk,) = _inproj(rot_kernel, u, w_in, l, lambda j: j, 2, tm, tn, "inproj_rot",
                         extra_specs=(tab_spec, tab_spec), extra_args=(cos_tab, sin_tab))
        cast_in, cast_out, cast_shapes = _cast_slab_specs(out_weights, l, gen_tiles * (m // tm), m // tm)
        proj, wr_bf, ws_bf, wo_bf = _inproj(gen_kernel, u, w_in, l, gen_map, gen_tiles, tm, tn, "inproj_gen",
                                            extra_specs=cast_in, extra_args=out_weights,
                                            extra_out_specs=cast_out, extra_out_shapes=cast_shapes)
        r = _retention(rqk, proj, gn_w3, l, s, t=1024)
        a = _swa(proj, attn_sinks[l], s, swa_q_w, swa_kv_w, q_blk=1, k_blk=24, v_blk=25, g_blk=3,
                 blocks_per_step=4)
        if l + 1 < depth:
            h, u = _outproj(r, a, proj, 2, 2, h, gate[l], wr_bf, ws_bf, wo_bf, s, tm=256,
                            nxt=(nw2[l + 1], scale[l + 1], shift[l + 1]))
        else:
            h = _outproj(r, a, proj, 2, 2, h, gate[l], wr_bf, ws_bf, wo_bf, s, tm=256,
                         final_w=final_norm_w.reshape(1, d))
    return h.reshape(b, s, d)
```

```python
import functools

import numpy as np
import jax
import jax.numpy as jnp
from jax import lax
from jax.experimental import pallas as pl
from jax.experimental.pallas import tpu as pltpu

F32 = jnp.float32
BF16 = jnp.bfloat16

RET_HEADS = 8
RET_ROT_BASE = 10000.0
SWA_HEAD_DIM = 64
SWA_KV_HEADS = 8
SWA_BLOCK = 128
EPS = 1e-6

RET_CHUNK = 256

V7X_VMEM_BYTES = 64 * 1024 * 1024
VMEM_LIMIT_BYTES = V7X_VMEM_BYTES - 6 * 1024 * 1024

NEG_BIG = -1e30


def _params(semantics):
    return pltpu.CompilerParams(dimension_semantics=semantics, vmem_limit_bytes=VMEM_LIMIT_BYTES)


def _sigmoid(y):
    return 0.5 * jnp.tanh(0.5 * y) + 0.5


def _mod_kernel(ct_ref, w_ref, b_ref, o_ref, *, batch):
    ct = ct_ref[...]
    ca = ct * _sigmoid(ct)
    w = w_ref[...]
    for b in range(batch):
        o_ref[b:b + 1, :] = jnp.sum(w * ca[:, b:b + 1], axis=0, keepdims=True) + b_ref[...]


def _modulation(c_t, ada_w, ada_b3, batch, tn):
    depth, d, n3 = ada_w.shape
    cols = c_t.shape[1]
    return pl.pallas_call(
        functools.partial(_mod_kernel, batch=batch),
        grid=(depth, n3 // tn),
        in_specs=[
            pl.BlockSpec((d, cols), lambda l, j: (0, 0)),
            pl.BlockSpec((None, d, tn), lambda l, j: (l, 0, j)),
            pl.BlockSpec((None, 1, tn), lambda l, j: (l, 0, j)),
        ],
        out_specs=pl.BlockSpec((None, batch, tn), lambda l, j: (l, 0, j)),
        out_shape=jax.ShapeDtypeStruct((depth, batch, n3), F32),
        compiler_params=_params(("parallel", "parallel")),
        name="adaln_modulation",
    )(c_t, ada_w, ada_b3)


def _modulated_norm(h, nw, scale, shift):
    ms = jnp.mean(h * h, axis=-1, keepdims=True)
    return h * lax.rsqrt(ms + EPS) * (nw * (1.0 + scale)) + shift


def _u_kernel(x_ref, nw_ref, sc_ref, sh_ref, u_ref):
    u_ref[...] = _modulated_norm(x_ref[...], nw_ref[...], sc_ref[...], sh_ref[...]).astype(BF16)


def _first_u(x2, nw, scale, shift, seq, tm):
    m, d = x2.shape
    per_batch = seq // tm
    return pl.pallas_call(
        _u_kernel,
        grid=(m // tm,),
        in_specs=[
            pl.BlockSpec((tm, d), lambda i: (i, 0)),
            pl.BlockSpec((1, d), lambda i: (0, 0)),
            pl.BlockSpec((None, 1, d), lambda i: (i // per_batch, 0, 0)),
            pl.BlockSpec((None, 1, d), lambda i: (i // per_batch, 0, 0)),
        ],
        out_specs=pl.BlockSpec((tm, d), lambda i: (i, 0)),
        out_shape=jax.ShapeDtypeStruct((m, d), BF16),
        compiler_params=_params(("parallel",)),
        name="first_modulated_norm",
    )(x2, nw, scale, shift)


def _inproj_rot_kernel(u_ref, w_ref, cos_ref, sin_ref, o_ref, wbf_ref, *, halves):
    @pl.when(pl.program_id(1) == 0)
    def _():
        wbf_ref[...] = w_ref[...].astype(BF16)

    hd = cos_ref.shape[-1]
    sub = u_ref.shape[0] // halves
    for part in range(halves):
        rows = slice(part * sub, (part + 1) * sub)
        y = jnp.dot(u_ref[rows, :], wbf_ref[...], preferred_element_type=F32)
        cs = cos_ref[rows, :]
        sn = sin_ref[rows, :]
        for h in range(y.shape[1] // hd):
            yh = y[:, h * hd:(h + 1) * hd]
            o_ref[rows, h * hd:(h + 1) * hd] = (yh * cs + pltpu.roll(yh, hd // 2, 1) * sn).astype(BF16)


def _inproj_gen_kernel(*refs, halves, silu_tiles, sigmoid_tiles, n_cast):
    u_ref, w_ref = refs[:2]
    cast_in = refs[2:2 + n_cast]
    o_ref = refs[2 + n_cast]
    cast_out = refs[3 + n_cast:3 + 2 * n_cast]
    wbf_ref = refs[3 + 2 * n_cast]

    @pl.when(pl.program_id(1) == 0)
    def _():
        wbf_ref[...] = w_ref[...].astype(BF16)

    j = pl.program_id(0)
    is_silu = jnp.logical_and(j >= silu_tiles[0], j < silu_tiles[1])
    is_sig = jnp.logical_and(j >= sigmoid_tiles[0], j < sigmoid_tiles[1])
    sub = u_ref.shape[0] // halves

    def body(epilogue):
        for src, dst in zip(cast_in, cast_out):
            dst[...] = src[...].astype(BF16)
        for part in range(halves):
            rows = slice(part * sub, (part + 1) * sub)
            y = jnp.dot(u_ref[rows, :], wbf_ref[...], preferred_element_type=F32)
            o_ref[rows, :] = epilogue(y).astype(BF16)

    @pl.when(is_silu)
    def _():
        body(lambda y: y * _sigmoid(y))

    @pl.when(is_sig)
    def _():
        body(_sigmoid)

    @pl.when(jnp.logical_not(jnp.logical_or(is_silu, is_sig)))
    def _():
        body(lambda y: y)


def _inproj(kernel_fn, u, w_in, layer, col_map, n_tiles, tm, tn, name, extra_specs=(), extra_args=(),
            extra_out_specs=(), extra_out_shapes=()):
    m, d = u.shape
    return pl.pallas_call(
        kernel_fn,
        grid=(n_tiles, m // tm),
        in_specs=[
            pl.BlockSpec((tm, d), lambda j, i: (i, 0)),
            pl.BlockSpec((None, d, tn), lambda j, i: (layer, 0, col_map(j))),
        ] + list(extra_specs),
        out_specs=[pl.BlockSpec((tm, tn), lambda j, i: (i, j))] + list(extra_out_specs),
        out_shape=[jax.ShapeDtypeStruct((m, n_tiles * tn), BF16)] + list(extra_out_shapes),
        scratch_shapes=[pltpu.VMEM((d, tn), BF16)],
        compiler_params=_params(("arbitrary", "arbitrary")),
        name=name,
    )(u, w_in, *extra_args)


def _cast_slab_specs(weights, layer, n_steps, row_tiles):
    n_slabs = 1 << ((n_steps // len(weights)).bit_length() - 1)
    in_specs, out_specs, out_shapes = [], [], []
    for idx, w in enumerate(weights):
        rows, cols = w.shape[1:]
        slab = rows // n_slabs
        assert slab * n_slabs == rows and slab % 16 == 0

        def slab_of(j, i, idx=idx):
            return jnp.clip(j * row_tiles + i - idx * n_slabs, 0, n_slabs - 1)

        in_specs.append(pl.BlockSpec((None, slab, cols), lambda j, i, f=slab_of: (layer, f(j, i), 0)))
        out_specs.append(pl.BlockSpec((slab, cols), lambda j, i, f=slab_of: (f(j, i), 0)))
        out_shapes.append(jax.ShapeDtypeStruct((rows, cols), BF16))
    return in_specs, out_specs, out_shapes


def _retention_kernel(qk_ref, v_ref, gnw_ref, o_ref, state_ref, *, heads, chunk, log_gammas):
    @pl.when(pl.program_id(1) == 0)
    def _():
        state_ref[...] = jnp.zeros_like(state_ref)

    t = qk_ref.shape[0]
    dk = qk_ref.shape[1] // (2 * heads)
    dv = v_ref.shape[1] // heads
    row = lax.broadcasted_iota(jnp.int32, (chunk, chunk), 0)
    col = lax.broadcasted_iota(jnp.int32, (chunk, chunk), 1)
    causal = row >= col
    pos = lax.broadcasted_iota(jnp.int32, (chunk, 1), 0).astype(F32)

    for h in range(heads):
        lg = log_gammas[h]
        q_scale = jnp.exp(lg * pos)
        k_scale = jnp.exp(-lg * pos)
        g_chunk = float(np.exp(lg * chunk))
        gnw = gnw_ref[:, h * dv:(h + 1) * dv]
        carry = state_ref[h]
        for c in range(t // chunk):
            rows = slice(c * chunk, (c + 1) * chunk)
            qa = (qk_ref[rows, h * dk:(h + 1) * dk].astype(F32) * q_scale).astype(BF16)
            kb = (qk_ref[rows, (heads + h) * dk:(heads + h + 1) * dk].astype(F32) * k_scale).astype(BF16)
            v = v_ref[rows, h * dv:(h + 1) * dv]
            s = lax.dot_general(qa, kb, (((1,), (1,)), ((), ())), preferred_element_type=F32)
            lhs = jnp.concatenate([jnp.where(causal, s, 0.0).astype(BF16), qa], axis=1)
            rhs = jnp.concatenate([v, carry.astype(BF16)], axis=0)
            o = jnp.dot(lhs, rhs, preferred_element_type=F32)
            kv = lax.dot_general(kb, v, (((0,), (0,)), ((), ())), preferred_element_type=F32)
            carry = (carry + kv) * g_chunk
            mu = jnp.mean(o, axis=-1, keepdims=True)
            dlt = o - mu
            var = jnp.mean(dlt * dlt, axis=-1, keepdims=True)
            o_ref[rows, h * dv:(h + 1) * dv] = (dlt * lax.rsqrt(var + EPS) * gnw).astype(BF16)
        state_ref[h] = carry


def _retention(rqk, proj, gn_w3, layer, seq, t, v_blk):
    m, qk_w = rqk.shape
    v_w = gn_w3.shape[-1]
    per_batch = seq // t
    log_gammas = tuple(float(np.log1p(-np.exp2(-5.0 - h))) for h in range(RET_HEADS))
    dk = qk_w // (2 * RET_HEADS)
    dv = v_w // RET_HEADS
    return pl.pallas_call(
        functools.partial(_retention_kernel, heads=RET_HEADS, chunk=RET_CHUNK, log_gammas=log_gammas),
        grid=(m // seq, per_batch),
        in_specs=[
            pl.BlockSpec((t, qk_w), lambda b, s: (b * per_batch + s, 0)),
            pl.BlockSpec((t, v_w), lambda b, s: (b * per_batch + s, v_blk)),
            pl.BlockSpec((None, 1, v_w), lambda b, s: (layer, 0, 0)),
        ],
        out_specs=pl.BlockSpec((t, v_w), lambda b, s: (b * per_batch + s, 0)),
        out_shape=jax.ShapeDtypeStruct((m, v_w), BF16),
        scratch_shapes=[pltpu.VMEM((RET_HEADS, dk, dv), F32)],
        compiler_params=_params(("parallel", "arbitrary")),
        name="retention",
    )(rqk, proj, gn_w3)


def _swa_body(sink_ref, q_ref, kp_ref, kc_ref, vp_ref, vc_ref, g_ref, o_ref, *, kv_heads, group, hd, blk,
              first):
    lanes = 2 * hd
    log2e = float(np.log2(np.e))
    lane = lax.broadcasted_iota(jnp.int32, (1, lanes), 1)
    lo_f = lane < hd
    scale = float(hd) ** -0.5 * log2e
    qmask = (jnp.where(lo_f, scale, 0.0).astype(BF16), jnp.where(lo_f, 0.0, scale).astype(BF16))
    key = lax.broadcasted_iota(jnp.int32, (blk, 2 * blk), 0)
    qry = jnp.bitwise_and(lax.broadcasted_iota(jnp.int32, (blk, 2 * blk), 1), blk - 1)
    upper = key > qry
    head_lo = lax.broadcasted_iota(jnp.int32, (1, 2 * blk), 1) < blk

    def dup(tile_bf16, parity):
        tile = tile_bf16.astype(F32)
        rolled = pltpu.roll(tile, hd, 1)
        keep_lo = lo_f if parity == 0 else jnp.logical_not(lo_f)
        return jnp.where(keep_lo, tile, rolled)

    def key_value_operands(k_tiles, v_tiles):
        ks = [dup(k_tiles[:, (h // 2) * lanes:(h // 2 + 1) * lanes], h % 2).astype(BF16)
              for h in range(kv_heads)]
        vts = [v_tiles[:, t * lanes:(t + 1) * lanes].astype(F32).T.astype(BF16)
               for t in range(kv_heads // 2)]
        return ks, vts

    nt = (((1,), (1,)), ((), ()))
    prev_ops = None if first else key_value_operands(kp_ref[...], vp_ref[...])
    for bi in range(q_ref.shape[0] // blk):
        rows = slice(bi * blk, (bi + 1) * blk)
        cur_ops = key_value_operands(kc_ref[rows, :], vc_ref[rows, :])
        no_prev = prev_ops is None
        for h in range(kv_heads):
            hrows = slice((h % 2) * hd, (h % 2 + 1) * hd)
            k_cur = cur_ops[0][h]
            vt_cur = cur_ops[1][h // 2]
            for gp in range(group // 2):
                qt = (h * group) // 2 + gp
                qcol = slice(qt * lanes, (qt + 1) * lanes)
                qtile = q_ref[rows, qcol]
                qq = jnp.concatenate([qtile * qmask[0], qtile * qmask[1]], axis=0)
                s_cur = lax.dot_general(k_cur, qq, nt, preferred_element_type=F32)
                if no_prev:
                    s = jnp.where(upper, NEG_BIG, s_cur)
                else:
                    s = jnp.where(upper, lax.dot_general(prev_ops[0][h], qq, nt, preferred_element_type=F32),
                                  s_cur)
                hq = h * group + 2 * gp
                sink = jnp.where(head_lo, sink_ref[hq], sink_ref[hq + 1]) * log2e
                mx = jnp.maximum(jnp.max(s, axis=0, keepdims=True), sink)
                p = jnp.exp2(s - mx)
                inv = 1.0 / (jnp.sum(p, axis=0, keepdims=True) + jnp.exp2(sink - mx))
                if no_prev:
                    ot = jnp.dot(vt_cur, p.astype(BF16), preferred_element_type=F32)
                else:
                    ot = (jnp.dot(prev_ops[1][h // 2], jnp.where(upper, p, 0.0).astype(BF16),
                                  preferred_element_type=F32)
                          + jnp.dot(vt_cur, jnp.where(upper, 0.0, p).astype(BF16), preferred_element_type=F32))
                ot = ot[hrows] * inv
                pair_t = jnp.concatenate([ot[:, :blk], ot[:, blk:]], axis=0)
                o_ref[rows, qcol] = (pair_t.T * g_ref[rows, qcol].astype(F32)).astype(BF16)
        prev_ops = cur_ops


def _swa_kernel(*refs, **kw):
    is_first = pl.program_id(1) == 0

    @pl.when(is_first)
    def _():
        _swa_body(*refs, first=True, **kw)

    @pl.when(jnp.logical_not(is_first))
    def _():
        _swa_body(*refs, first=False, **kw)


def _swa(proj, sinks_l, seq, q_w, kv_w, q_blk, k_blk, v_blk, g_blk, blocks_per_step):
    m = proj.shape[0]
    t = blocks_per_step * SWA_BLOCK
    steps = seq // t
    kv_heads = SWA_KV_HEADS
    group = q_w // SWA_HEAD_DIM // kv_heads
    cur = lambda b, n: b * steps + n
    prev = lambda b, n: (b * steps + n) * blocks_per_step - jnp.minimum(n, 1)
    return pl.pallas_call(
        functools.partial(_swa_kernel, kv_heads=kv_heads, group=group, hd=SWA_HEAD_DIM, blk=SWA_BLOCK),
        grid=(m // seq, steps),
        in_specs=[
            pl.BlockSpec(memory_space=pltpu.SMEM),
            pl.BlockSpec((t, q_w), lambda b, n: (cur(b, n), q_blk)),
            pl.BlockSpec((SWA_BLOCK, kv_w), lambda b, n: (prev(b, n), k_blk)),
            pl.BlockSpec((t, kv_w), lambda b, n: (cur(b, n), k_blk)),
            pl.BlockSpec((SWA_BLOCK, kv_w), lambda b, n: (prev(b, n), v_blk)),
            pl.BlockSpec((t, kv_w), lambda b, n: (cur(b, n), v_blk)),
            pl.BlockSpec((t, q_w), lambda b, n: (cur(b, n), g_blk)),
        ],
        out_specs=pl.BlockSpec((t, q_w), lambda b, n: (cur(b, n), 0)),
        out_shape=jax.ShapeDtypeStruct((m, q_w), BF16),
        compiler_params=_params(("parallel", "arbitrary")),
        name="swa",
    )(sinks_l, proj, proj, proj, proj, proj, proj)


def _outproj_kernel(*refs, last):
    if last:
        r_ref, rg_ref, a_ref, sg_ref, h_ref, gate_ref, wr_ref, ws_ref, wo_ref, fw_ref, out_ref = refs
    else:
        (r_ref, rg_ref, a_ref, sg_ref, h_ref, gate_ref, wr_ref, ws_ref, wo_ref,
         nw_ref, sc_ref, sh_ref, hn_ref, u_ref) = refs
    d = h_ref.shape[1]
    y1 = jnp.dot(r_ref[...] * rg_ref[...], wr_ref[...], preferred_element_type=F32)
    y2 = jnp.dot(a_ref[...], ws_ref[...], preferred_element_type=F32)
    merged = sg_ref[:, :d].astype(F32) * y1 + sg_ref[:, d:].astype(F32) * y2
    z = jnp.dot(merged.astype(BF16), wo_ref[...], preferred_element_type=F32)
    hn = h_ref[...] + gate_ref[...] * z
    if last:
        ms = jnp.mean(hn * hn, axis=-1, keepdims=True)
        out_ref[...] = hn * lax.rsqrt(ms + EPS) * fw_ref[...]
    else:
        hn_ref[...] = hn
        u_ref[...] = _modulated_norm(hn, nw_ref[...], sc_ref[...], sh_ref[...]).astype(BF16)


def _outproj(r, a, proj, rg_blk, sig_blk, h, gate, wr, ws, wo, seq, tm, nxt=None, final_w=None):
    m, d = h.shape
    per_batch = seq // tm
    last = nxt is None
    row = lambda i: (i, 0)
    vec = pl.BlockSpec((None, 1, d), lambda i: (i // per_batch, 0, 0))
    wspec = lambda rows: pl.BlockSpec((rows, d), lambda i: (0, 0), pipeline_mode=pl.Buffered(1))
    in_specs = [
        pl.BlockSpec((tm, r.shape[1]), row), pl.BlockSpec((tm, r.shape[1]), lambda i: (i, rg_blk)),
        pl.BlockSpec((tm, d), row), pl.BlockSpec((tm, 2 * d), lambda i: (i, sig_blk)),
        pl.BlockSpec((tm, d), row), vec, wspec(r.shape[1]), wspec(d), wspec(d),
    ]
    args = [r, proj, a, proj, h, gate, wr, ws, wo]
    if last:
        in_specs.append(pl.BlockSpec((1, d), lambda i: (0, 0)))
        args.append(final_w)
        out_specs = pl.BlockSpec((tm, d), row)
        out_shape = jax.ShapeDtypeStruct((m, d), F32)
    else:
        nw, sc, sh = nxt
        in_specs += [pl.BlockSpec((1, d), lambda i: (0, 0)), vec, vec]
        args += [nw, sc, sh]
        out_specs = (pl.BlockSpec((tm, d), row), pl.BlockSpec((tm, d), row))
        out_shape = (jax.ShapeDtypeStruct((m, d), F32), jax.ShapeDtypeStruct((m, d), BF16))
    return pl.pallas_call(
        functools.partial(_outproj_kernel, last=last),
        grid=(m // tm,),
        in_specs=in_specs,
        out_specs=out_specs,
        out_shape=out_shape,
        compiler_params=_params(("parallel",)),
        name="outproj_last" if last else "outproj",
    )(*args)


def kernel(x, c, norm_w, ada_w, ada_b, w_in, ret_gn_w, attn_sinks, w_ret_o, w_swa_o, w_out, final_norm_w):
    b, s, d = x.shape
    depth = norm_w.shape[0]
    m = b * s
    ret_v_w = ret_gn_w.shape[1]
    ret_qk_w = ret_v_w // 2
    swa_q_w = d
    swa_kv_w = SWA_KV_HEADS * SWA_HEAD_DIM
    tn = 1024
    tm = 2048
    assert w_in.shape[2] == 2 * ret_qk_w + 2 * ret_v_w + 2 * swa_q_w + 2 * swa_kv_w + 2 * d
    assert ret_qk_w == tn and 2 * swa_kv_w == tn and ret_v_w == 2 * tn and d == 2 * tn and s % tm == 0

    c_t = jnp.pad(c.T, ((0, 0), (0, 8 - b)))
    mod = _modulation(c_t, ada_w, ada_b.reshape(depth, 1, 3 * d), b, tn=1536)
    shift = mod[:, :, :d].reshape(depth, b, 1, d)
    scale = mod[:, :, d:2 * d].reshape(depth, b, 1, d)
    gate = mod[:, :, 2 * d:].reshape(depth, b, 1, d)

    half = ret_qk_w // RET_HEADS // 2
    inv = (1.0 / (RET_ROT_BASE ** np.linspace(0.0, 1.0, half, dtype=np.float32))).astype(np.float32)
    ang = np.arange(s, dtype=np.float32)[:, None] * inv[None, :]
    cos2 = np.concatenate([np.cos(ang), np.cos(ang)], axis=-1)
    sin2 = np.concatenate([-np.sin(ang), np.sin(ang)], axis=-1)
    k_scale = np.float32(float(2 * half) ** -0.5)
    cos_tab = jnp.asarray(np.stack([cos2, cos2 * k_scale]), dtype=F32)
    sin_tab = jnp.asarray(np.stack([sin2, sin2 * k_scale]), dtype=F32)

    gn_w3 = ret_gn_w.reshape(depth, 1, ret_v_w)
    nw2 = norm_w.reshape(depth, 1, d)

    h = x.reshape(m, d)
    u = _first_u(h, nw2[0], scale[0], shift[0], s, tm=1024)

    gen_map = lambda j: jnp.where(j < 2, j + 2, jnp.where(j < 4, j + 4, jnp.where(j < 6, j,
                                  jnp.where(j < 12, j + 3, 8))))
    gen_tiles = 13
    off_rv, off_sq, off_rg, off_sg, off_sig, off_sk = (i * tn for i in (0, 2, 4, 6, 8, 12))
    off_sv = off_sk + swa_kv_w
    out_weights = (w_ret_o, w_swa_o, w_out)
    gen_kernel = functools.partial(_inproj_gen_kernel, halves=2, silu_tiles=(4, 8), sigmoid_tiles=(8, 12),
                                   n_cast=len(out_weights))
    rot_kernel = functools.partial(_inproj_rot_kernel, halves=2)
    per_batch = s // tm
    hd2 = cos_tab.shape[-1]
    tab_spec = pl.BlockSpec((None, tm, hd2), lambda j, i: (j, i % per_batch, 0))

    for l in range(depth):
        (rqk,) = _inproj(rot_kernel, u, w_in, l, lambda j: j, 2, tm, tn, "inproj_rot",
                         extra_specs=(tab_spec, tab_spec), extra_args=(cos_tab, sin_tab))
        cast_in, cast_out, cast_shapes = _cast_slab_specs(out_weights, l, gen_tiles * (m // tm), m // tm)
        proj, wr_bf, ws_bf, wo_bf = _inproj(gen_kernel, u, w_in, l, gen_map, gen_tiles, tm, tn, "inproj_gen",
                                            extra_specs=cast_in, extra_args=out_weights,
                                            extra_out_specs=cast_out, extra_out_shapes=cast_shapes)
        r = _retention(rqk, proj, gn_w3, l, s, t=1024, v_blk=off_rv // ret_v_w)
        a = _swa(proj, attn_sinks[l], s, swa_q_w, swa_kv_w, q_blk=off_sq // swa_q_w, k_blk=off_sk // swa_kv_w,
                 v_blk=off_sv // swa_kv_w, g_blk=off_sg // swa_q_w, blocks_per_step=4)
        out_args = (r, a, proj, off_rg // ret_v_w, off_sig // (2 * d), h, gate[l], wr_bf, ws_bf, wo_bf, s)
        if l + 1 < depth:
            h, u = _outproj(*out_args, tm=256, nxt=(nw2[l + 1], scale[l + 1], shift[l + 1]))
        else:
            h = _outproj(*out_args, tm=256, final_w=final_norm_w.reshape(1, d))
    return h.reshape(b, s, d)
```

```python
import functools

import numpy as np
import jax
import jax.numpy as jnp
from jax import lax
from jax.experimental import pallas as pl
from jax.experimental.pallas import tpu as pltpu

F32 = jnp.float32
BF16 = jnp.bfloat16

RET_HEADS = 8
RET_ROT_BASE = 10000.0
SWA_HEAD_DIM = 64
SWA_KV_HEADS = 8
SWA_BLOCK = 128
EPS = 1e-6

RET_CHUNK = 256

V7X_VMEM_BYTES = 64 * 1024 * 1024
VMEM_LIMIT_BYTES = V7X_VMEM_BYTES - 6 * 1024 * 1024

NEG_BIG = -1e30


def _params(semantics):
    return pltpu.CompilerParams(dimension_semantics=semantics, vmem_limit_bytes=VMEM_LIMIT_BYTES)


def _sigmoid(y):
    return 0.5 * jnp.tanh(0.5 * y) + 0.5


def _mod_kernel(ct_ref, w_ref, b_ref, o_ref, *, batch):
    ct = ct_ref[...]
    ca = ct * _sigmoid(ct)
    w = w_ref[...]
    for b in range(batch):
        o_ref[b:b + 1, :] = jnp.sum(w * ca[:, b:b + 1], axis=0, keepdims=True) + b_ref[...]


def _modulation(c_t, ada_w, ada_b3, batch, tn):
    depth, d, n3 = ada_w.shape
    cols = c_t.shape[1]
    return pl.pallas_call(
        functools.partial(_mod_kernel, batch=batch),
        grid=(depth, n3 // tn),
        in_specs=[
            pl.BlockSpec((d, cols), lambda l, j: (0, 0)),
            pl.BlockSpec((None, d, tn), lambda l, j: (l, 0, j)),
            pl.BlockSpec((None, 1, tn), lambda l, j: (l, 0, j)),
        ],
        out_specs=pl.BlockSpec((None, batch, tn), lambda l, j: (l, 0, j)),
        out_shape=jax.ShapeDtypeStruct((depth, batch, n3), F32),
        compiler_params=_params(("parallel", "parallel")),
        name="adaln_modulation",
    )(c_t, ada_w, ada_b3)


def _modulated_norm(h, nw, scale, shift):
    ms = jnp.mean(h * h, axis=-1, keepdims=True)
    return h * lax.rsqrt(ms + EPS) * (nw * (1.0 + scale)) + shift


def _u_kernel(x_ref, nw_ref, sc_ref, sh_ref, u_ref):
    u_ref[...] = _modulated_norm(x_ref[...], nw_ref[...], sc_ref[...], sh_ref[...]).astype(BF16)


def _first_u(x2, nw, scale, shift, seq, tm):
    m, d = x2.shape
    per_batch = seq // tm
    return pl.pallas_call(
        _u_kernel,
        grid=(m // tm,),
        in_specs=[
            pl.BlockSpec((tm, d), lambda i: (i, 0)),
            pl.BlockSpec((1, d), lambda i: (0, 0)),
            pl.BlockSpec((None, 1, d), lambda i: (i // per_batch, 0, 0)),
            pl.BlockSpec((None, 1, d), lambda i: (i // per_batch, 0, 0)),
        ],
        out_specs=pl.BlockSpec((tm, d), lambda i: (i, 0)),
        out_shape=jax.ShapeDtypeStruct((m, d), BF16),
        compiler_params=_params(("parallel",)),
        name="first_modulated_norm",
    )(x2, nw, scale, shift)


def _inproj_rot_kernel(u_ref, w_ref, cos_ref, sin_ref, o_ref, wbf_ref, *, halves):
    @pl.when(pl.program_id(1) == 0)
    def _():
        wbf_ref[...] = w_ref[...].astype(BF16)

    hd = cos_ref.shape[-1]
    sub = u_ref.shape[0] // halves
    for part in range(halves):
        rows = slice(part * sub, (part + 1) * sub)
        y = jnp.dot(u_ref[rows, :], wbf_ref[...], preferred_element_type=F32)
        cs = cos_ref[rows, :]
        sn = sin_ref[rows, :]
        for h in range(y.shape[1] // hd):
            yh = y[:, h * hd:(h + 1) * hd]
            o_ref[rows, h * hd:(h + 1) * hd] = (yh * cs + pltpu.roll(yh, hd // 2, 1) * sn).astype(BF16)


def _inproj_gen_kernel(*refs, halves, silu_tiles, sigmoid_tiles, n_cast):
    u_ref, w_ref = refs[:2]
    cast_in = refs[2:2 + n_cast]
    o_ref = refs[2 + n_cast]
    cast_out = refs[3 + n_cast:3 + 2 * n_cast]
    wbf_ref = refs[3 + 2 * n_cast]

    @pl.when(pl.program_id(1) == 0)
    def _():
        wbf_ref[...] = w_ref[...].astype(BF16)

    j = pl.program_id(0)
    is_silu = jnp.logical_and(j >= silu_tiles[0], j < silu_tiles[1])
    is_sig = jnp.logical_and(j >= sigmoid_tiles[0], j < sigmoid_tiles[1])
    sub = u_ref.shape[0] // halves

    def body(epilogue):
        for src, dst in zip(cast_in, cast_out):
            dst[...] = src[...].astype(BF16)
        for part in range(halves):
            rows = slice(part * sub, (part + 1) * sub)
            y = jnp.dot(u_ref[rows, :], wbf_ref[...], preferred_element_type=F32)
            o_ref[rows, :] = epilogue(y).astype(BF16)

    @pl.when(is_silu)
    def _():
        body(lambda y: y * _sigmoid(y))

    @pl.when(is_sig)
    def _():
        body(_sigmoid)

    @pl.when(jnp.logical_not(jnp.logical_or(is_silu, is_sig)))
    def _():
        body(lambda y: y)


def _inproj(kernel_fn, u, w_in, layer, col_map, n_tiles, tm, tn, name, extra_specs=(), extra_args=(),
            extra_out_specs=(), extra_out_shapes=()):
    m, d = u.shape
    return pl.pallas_call(
        kernel_fn,
        grid=(n_tiles, m // tm),
        in_specs=[
            pl.BlockSpec((tm, d), lambda j, i: (i, 0)),
            pl.BlockSpec((None, d, tn), lambda j, i: (layer, 0, col_map(j))),
        ] + list(extra_specs),
        out_specs=[pl.BlockSpec((tm, tn), lambda j, i: (i, j))] + list(extra_out_specs),
        out_shape=[jax.ShapeDtypeStruct((m, n_tiles * tn), BF16)] + list(extra_out_shapes),
        scratch_shapes=[pltpu.VMEM((d, tn), BF16)],
        compiler_params=_params(("arbitrary", "arbitrary")),
        name=name,
    )(u, w_in, *extra_args)


def _cast_slab_specs(weights, layer, n_steps, row_tiles):
    n_slabs = 1 << ((n_steps // len(weights)).bit_length() - 1)
    in_specs, out_specs, out_shapes = [], [], []
    for idx, w in enumerate(weights):
        rows, cols = w.shape[1:]
        slab = rows // n_slabs
        assert slab * n_slabs == rows and slab % 16 == 0

        def slab_of(j, i, idx=idx):
            return jnp.clip(j * row_tiles + i - idx * n_slabs, 0, n_slabs - 1)

        in_specs.append(pl.BlockSpec((None, slab, cols), lambda j, i, f=slab_of: (layer, f(j, i), 0)))
        out_specs.append(pl.BlockSpec((slab, cols), lambda j, i, f=slab_of: (f(j, i), 0)))
        out_shapes.append(jax.ShapeDtypeStruct((rows, cols), BF16))
    return in_specs, out_specs, out_shapes


def _retention_heads(qk_ref, v_ref, gnw_ref, o_ref, state_ref, head_list, *, heads, chunk, log_gammas):
    t = qk_ref.shape[0]
    dk = qk_ref.shape[1] // (2 * heads)
    dv = v_ref.shape[1] // heads
    row = lax.broadcasted_iota(jnp.int32, (chunk, chunk), 0)
    col = lax.broadcasted_iota(jnp.int32, (chunk, chunk), 1)
    causal = row >= col
    pos = lax.broadcasted_iota(jnp.int32, (chunk, 1), 0).astype(F32)

    for h in head_list:
        lg = log_gammas[h]
        q_scale = jnp.exp(lg * pos)
        k_scale = jnp.exp(-lg * pos)
        g_chunk = float(np.exp(lg * chunk))
        gnw = gnw_ref[:, h * dv:(h + 1) * dv]
        carry = state_ref[h]
        for c in range(t // chunk):
            rows = slice(c * chunk, (c + 1) * chunk)
            qa = (qk_ref[rows, h * dk:(h + 1) * dk].astype(F32) * q_scale).astype(BF16)
            kb = (qk_ref[rows, (heads + h) * dk:(heads + h + 1) * dk].astype(F32) * k_scale).astype(BF16)
            v = v_ref[rows, h * dv:(h + 1) * dv]
            s = lax.dot_general(qa, kb, (((1,), (1,)), ((), ())), preferred_element_type=F32)
            lhs = jnp.concatenate([jnp.where(causal, s, 0.0).astype(BF16), qa], axis=1)
            rhs = jnp.concatenate([v, carry.astype(BF16)], axis=0)
            o = jnp.dot(lhs, rhs, preferred_element_type=F32)
            kv = lax.dot_general(kb, v, (((0,), (0,)), ((), ())), preferred_element_type=F32)
            carry = (carry + kv) * g_chunk
            mu = jnp.mean(o, axis=-1, keepdims=True)
            dlt = o - mu
            var = jnp.mean(dlt * dlt, axis=-1, keepdims=True)
            o_ref[rows, h * dv:(h + 1) * dv] = (dlt * lax.rsqrt(var + EPS) * gnw).astype(BF16)
        state_ref[h] = carry


def _swa_body(sink_ref, q_ref, kp_ref, kc_ref, vp_ref, vc_ref, g_ref, o_ref, *, kv_heads, group, hd, blk,
              first):
    lanes = 2 * hd
    log2e = float(np.log2(np.e))
    lane = lax.broadcasted_iota(jnp.int32, (1, lanes), 1)
    lo_f = lane < hd
    scale = float(hd) ** -0.5 * log2e
    qmask = (jnp.where(lo_f, scale, 0.0).astype(BF16), jnp.where(lo_f, 0.0, scale).astype(BF16))
    key = lax.broadcasted_iota(jnp.int32, (blk, 2 * blk), 0)
    qry = jnp.bitwise_and(lax.broadcasted_iota(jnp.int32, (blk, 2 * blk), 1), blk - 1)
    upper = key > qry
    head_lo = lax.broadcasted_iota(jnp.int32, (1, 2 * blk), 1) < blk

    def dup(tile_bf16, parity):
        tile = tile_bf16.astype(F32)
        rolled = pltpu.roll(tile, hd, 1)
        keep_lo = lo_f if parity == 0 else jnp.logical_not(lo_f)
        return jnp.where(keep_lo, tile, rolled)

    def key_value_operands(k_tiles, v_tiles):
        ks = [dup(k_tiles[:, (h // 2) * lanes:(h // 2 + 1) * lanes], h % 2).astype(BF16)
              for h in range(kv_heads)]
        vts = [v_tiles[:, t * lanes:(t + 1) * lanes].astype(F32).T.astype(BF16)
               for t in range(kv_heads // 2)]
        return ks, vts

    nt = (((1,), (1,)), ((), ()))
    prev_ops = None if first else key_value_operands(kp_ref[...], vp_ref[...])
    for bi in range(q_ref.shape[0] // blk):
        rows = slice(bi * blk, (bi + 1) * blk)
        cur_ops = key_value_operands(kc_ref[rows, :], vc_ref[rows, :])
        no_prev = prev_ops is None
        for h in range(kv_heads):
            hrows = slice((h % 2) * hd, (h % 2 + 1) * hd)
            k_cur = cur_ops[0][h]
            vt_cur = cur_ops[1][h // 2]
            for gp in range(group // 2):
                qt = (h * group) // 2 + gp
                qcol = slice(qt * lanes, (qt + 1) * lanes)
                qtile = q_ref[rows, qcol]
                qq = jnp.concatenate([qtile * qmask[0], qtile * qmask[1]], axis=0)
                s_cur = lax.dot_general(k_cur, qq, nt, preferred_element_type=F32)
                if no_prev:
                    s = jnp.where(upper, NEG_BIG, s_cur)
                else:
                    s = jnp.where(upper, lax.dot_general(prev_ops[0][h], qq, nt, preferred_element_type=F32),
                                  s_cur)
                hq = h * group + 2 * gp
                sink = jnp.where(head_lo, sink_ref[hq], sink_ref[hq + 1]) * log2e
                mx = jnp.maximum(jnp.max(s, axis=0, keepdims=True), sink)
                p = jnp.exp2(s - mx)
                inv = 1.0 / (jnp.sum(p, axis=0, keepdims=True) + jnp.exp2(sink - mx))
                if no_prev:
                    ot = jnp.dot(vt_cur, p.astype(BF16), preferred_element_type=F32)
                else:
                    ot = (jnp.dot(prev_ops[1][h // 2], jnp.where(upper, p, 0.0).astype(BF16),
                                  preferred_element_type=F32)
                          + jnp.dot(vt_cur, jnp.where(upper, 0.0, p).astype(BF16), preferred_element_type=F32))
                ot = ot[hrows] * inv
                pair_t = jnp.concatenate([ot[:, :blk], ot[:, blk:]], axis=0)
                o_ref[rows, qcol] = (pair_t.T * g_ref[rows, qcol].astype(F32)).astype(BF16)
        prev_ops = cur_ops


def _swa_kernel(*refs, **kw):
    is_first = pl.program_id(1) == 0

    @pl.when(is_first)
    def _():
        _swa_body(*refs, first=True, **kw)

    @pl.when(jnp.logical_not(is_first))
    def _():
        _swa_body(*refs, first=False, **kw)


def _swa(proj, sinks_l, seq, q_w, kv_w, q_blk, k_blk, v_blk, g_blk, blocks_per_step):
    m = proj.shape[0]
    t = blocks_per_step * SWA_BLOCK
    steps = seq // t
    kv_heads = SWA_KV_HEADS
    group = q_w // SWA_HEAD_DIM // kv_heads
    cur = lambda b, n: b * steps + n
    prev = lambda b, n: (b * steps + n) * blocks_per_step - jnp.minimum(n, 1)
    return pl.pallas_call(
        functools.partial(_swa_kernel, kv_heads=kv_heads, group=group, hd=SWA_HEAD_DIM, blk=SWA_BLOCK),
        grid=(m // seq, steps),
        in_specs=[
            pl.BlockSpec(memory_space=pltpu.SMEM),
            pl.BlockSpec((t, q_w), lambda b, n: (cur(b, n), q_blk)),
            pl.BlockSpec((SWA_BLOCK, kv_w), lambda b, n: (prev(b, n), k_blk)),
            pl.BlockSpec((t, kv_w), lambda b, n: (cur(b, n), k_blk)),
            pl.BlockSpec((SWA_BLOCK, kv_w), lambda b, n: (prev(b, n), v_blk)),
            pl.BlockSpec((t, kv_w), lambda b, n: (cur(b, n), v_blk)),
            pl.BlockSpec((t, q_w), lambda b, n: (cur(b, n), g_blk)),
        ],
        out_specs=pl.BlockSpec((t, q_w), lambda b, n: (cur(b, n), 0)),
        out_shape=jax.ShapeDtypeStruct((m, q_w), BF16),
        compiler_params=_params(("parallel", "arbitrary")),
        name="swa",
    )(sinks_l, proj, proj, proj, proj, proj, proj)


def _outproj_kernel(*refs, last, ret_kw, tiles_per_seq):
    (qk0_ref, rv0_ref, qk_ref, rv_ref, gnw_ref, rg_ref, a_ref, sg_ref, h_ref, gate_ref,
     wr_ref, ws_ref, wo_ref) = refs[:13]
    if last:
        fw_ref, out_ref, rbuf_ref, state_ref = refs[13:]
    else:
        nw_ref, sc_ref, sh_ref, hn_ref, u_ref, rbuf_ref, state_ref = refs[13:]
    d = h_ref.shape[1]
    i = pl.program_id(0)
    heads = ret_kw["heads"]
    slot = lax.rem(i, 2)

    @pl.when(i == 0)
    def _():
        state_ref[...] = jnp.zeros_like(state_ref)
        _retention_heads(qk0_ref, rv0_ref, gnw_ref, rbuf_ref.at[0], state_ref, range(heads), **ret_kw)

    @pl.when(lax.rem(i + 1, tiles_per_seq) == 0)
    def _():
        state_ref[...] = jnp.zeros_like(state_ref)

    nxt_buf = rbuf_ref.at[1 - slot]
    y1 = jnp.dot(rbuf_ref[slot] * rg_ref[...], wr_ref[...], preferred_element_type=F32)
    _retention_heads(qk_ref, rv_ref, gnw_ref, nxt_buf, state_ref, range(0, heads // 2), **ret_kw)
    y2 = jnp.dot(a_ref[...], ws_ref[...], preferred_element_type=F32)
    _retention_heads(qk_ref, rv_ref, gnw_ref, nxt_buf, state_ref, range(heads // 2, heads), **ret_kw)
    merged = sg_ref[:, :d].astype(F32) * y1 + sg_ref[:, d:].astype(F32) * y2
    z = jnp.dot(merged.astype(BF16), wo_ref[...], preferred_element_type=F32)
    hn = h_ref[...] + gate_ref[...] * z
    if last:
        ms = jnp.mean(hn * hn, axis=-1, keepdims=True)
        out_ref[...] = hn * lax.rsqrt(ms + EPS) * fw_ref[...]
    else:
        hn_ref[...] = hn
        u_ref[...] = _modulated_norm(hn, nw_ref[...], sc_ref[...], sh_ref[...]).astype(BF16)


def _outproj(rqk, gn_w3, layer, a, proj, rv_blk, rg_blk, sig_blk, h, gate, wr, ws, wo, seq, tm,
             nxt=None, final_w=None):
    m, d = h.shape
    qk_w = rqk.shape[1]
    v_w = gn_w3.shape[-1]
    n = m // tm
    per_batch = seq // tm
    assert tm == RET_CHUNK
    last = nxt is None
    row = lambda i: (i, 0)
    nxt_row = lambda i: jnp.minimum(i + 1, n - 1)
    vec = pl.BlockSpec((None, 1, d), lambda i: (i // per_batch, 0, 0))
    wspec = lambda rows: pl.BlockSpec((rows, d), lambda i: (0, 0), pipeline_mode=pl.Buffered(1))
    in_specs = [
        pl.BlockSpec((tm, qk_w), lambda i: (0, 0)), pl.BlockSpec((tm, v_w), lambda i: (0, rv_blk)),
        pl.BlockSpec((tm, qk_w), lambda i: (nxt_row(i), 0)),
        pl.BlockSpec((tm, v_w), lambda i: (nxt_row(i), rv_blk)),
        pl.BlockSpec((None, 1, v_w), lambda i: (layer, 0, 0)),
        pl.BlockSpec((tm, v_w), lambda i: (i, rg_blk)),
        pl.BlockSpec((tm, d), row), pl.BlockSpec((tm, 2 * d), lambda i: (i, sig_blk)),
        pl.BlockSpec((tm, d), row), vec, wspec(v_w), wspec(d), wspec(d),
    ]
    args = [rqk, proj, rqk, proj, gn_w3, proj, a, proj, h, gate, wr, ws, wo]
    log_gammas = tuple(float(np.log1p(-np.exp2(-5.0 - hh))) for hh in range(RET_HEADS))
    ret_kw = dict(heads=RET_HEADS, chunk=RET_CHUNK, log_gammas=log_gammas)
    scratch = [pltpu.VMEM((2, tm, v_w), BF16),
               pltpu.VMEM((RET_HEADS, qk_w // (2 * RET_HEADS), v_w // RET_HEADS), F32)]
    if last:
        in_specs.append(pl.BlockSpec((1, d), lambda i: (0, 0)))
        args.append(final_w)
        out_specs = pl.BlockSpec((tm, d), row)
        out_shape = jax.ShapeDtypeStruct((m, d), F32)
    else:
        nw, sc, sh = nxt
        in_specs += [pl.BlockSpec((1, d), lambda i: (0, 0)), vec, vec]
        args += [nw, sc, sh]
        out_specs = (pl.BlockSpec((tm, d), row), pl.BlockSpec((tm, d), row))
        out_shape = (jax.ShapeDtypeStruct((m, d), F32), jax.ShapeDtypeStruct((m, d), BF16))
    return pl.pallas_call(
        functools.partial(_outproj_kernel, last=last, ret_kw=ret_kw, tiles_per_seq=per_batch),
        grid=(n,),
        in_specs=in_specs,
        out_specs=out_specs,
        out_shape=out_shape,
        scratch_shapes=scratch,
        compiler_params=_params(("arbitrary",)),
        name="outproj_last" if last else "outproj",
    )(*args)


def kernel(x, c, norm_w, ada_w, ada_b, w_in, ret_gn_w, attn_sinks, w_ret_o, w_swa_o, w_out, final_norm_w):
    b, s, d = x.shape
    depth = norm_w.shape[0]
    m = b * s
    ret_v_w = ret_gn_w.shape[1]
    ret_qk_w = ret_v_w // 2
    swa_q_w = d
    swa_kv_w = SWA_KV_HEADS * SWA_HEAD_DIM
    tn = 1024
    tm = 2048
    assert w_in.shape[2] == 2 * ret_qk_w + 2 * ret_v_w + 2 * swa_q_w + 2 * swa_kv_w + 2 * d
    assert ret_qk_w == tn and 2 * swa_kv_w == tn and ret_v_w == 2 * tn and d == 2 * tn and s % tm == 0

    c_t = jnp.pad(c.T, ((0, 0), (0, 8 - b)))
    mod = _modulation(c_t, ada_w, ada_b.reshape(depth, 1, 3 * d), b, tn=1536)
    shift = mod[:, :, :d].reshape(depth, b, 1, d)
    scale = mod[:, :, d:2 * d].reshape(depth, b, 1, d)
    gate = mod[:, :, 2 * d:].reshape(depth, b, 1, d)

    half = ret_qk_w // RET_HEADS // 2
    inv = (1.0 / (RET_ROT_BASE ** np.linspace(0.0, 1.0, half, dtype=np.float32))).astype(np.float32)
    ang = np.arange(s, dtype=np.float32)[:, None] * inv[None, :]
    cos2 = np.concatenate([np.cos(ang), np.cos(ang)], axis=-1)
    sin2 = np.concatenate([-np.sin(ang), np.sin(ang)], axis=-1)
    k_scale = np.float32(float(2 * half) ** -0.5)
    cos_tab = jnp.asarray(np.stack([cos2, cos2 * k_scale]), dtype=F32)
    sin_tab = jnp.asarray(np.stack([sin2, sin2 * k_scale]), dtype=F32)

    gn_w3 = ret_gn_w.reshape(depth, 1, ret_v_w)
    nw2 = norm_w.reshape(depth, 1, d)

    h = x.reshape(m, d)
    u = _first_u(h, nw2[0], scale[0], shift[0], s, tm=1024)

    gen_map = lambda j: jnp.where(j < 2, j + 2, jnp.where(j < 4, j + 4, jnp.where(j < 6, j,
                                  jnp.where(j < 12, j + 3, 8))))
    gen_tiles = 13
    off_rv, off_sq, off_rg, off_sg, off_sig, off_sk = (i * tn for i in (0, 2, 4, 6, 8, 12))
    off_sv = off_sk + swa_kv_w
    out_weights = (w_ret_o, w_swa_o, w_out)
    gen_kernel = functools.partial(_inproj_gen_kernel, halves=2, silu_tiles=(4, 8), sigmoid_tiles=(8, 12),
                                   n_cast=len(out_weights))
    rot_kernel = functools.partial(_inproj_rot_kernel, halves=2)
    per_batch = s // tm
    hd2 = cos_tab.shape[-1]
    tab_spec = pl.BlockSpec((None, tm, hd2), lambda j, i: (j, i % per_batch, 0))

    for l in range(depth):
        (rqk,) = _inproj(rot_kernel, u, w_in, l, lambda j: j, 2, tm, tn, "inproj_rot",
                         extra_specs=(tab_spec, tab_spec), extra_args=(cos_tab, sin_tab))
        cast_in, cast_out, cast_shapes = _cast_slab_specs(out_weights, l, gen_tiles * (m // tm), m // tm)
        proj, wr_bf, ws_bf, wo_bf = _inproj(gen_kernel, u, w_in, l, gen_map, gen_tiles, tm, tn, "inproj_gen",
                                            extra_specs=cast_in, extra_args=out_weights,
                                            extra_out_specs=cast_out, extra_out_shapes=cast_shapes)
        a = _swa(proj, attn_sinks[l], s, swa_q_w, swa_kv_w, q_blk=off_sq // swa_q_w, k_blk=off_sk // swa_kv_w,
                 v_blk=off_sv // swa_kv_w, g_blk=off_sg // swa_q_w, blocks_per_step=4)
        out_args = (rqk, gn_w3, l, a, proj, off_rv // ret_v_w, off_rg // ret_v_w, off_sig // (2 * d),
                    h, gate[l], wr_bf, ws_bf, wo_bf, s)
        if l + 1 < depth:
            h, u = _outproj(*out_args, tm=256, nxt=(nw2[l + 1], scale[l + 1], shift[l + 1]))
        else:
            h = _outproj(*out_args, tm=256, final_w=final_norm_w.reshape(1, d))
    return h.reshape(b, s, d)
```

```python
import functools

import numpy as np
import jax
import jax.numpy as jnp
from jax import lax
from jax.experimental import pallas as pl
from jax.experimental.pallas import tpu as pltpu

F32 = jnp.float32
BF16 = jnp.bfloat16

RET_HEADS = 8
RET_ROT_BASE = 10000.0
SWA_HEAD_DIM = 64
SWA_KV_HEADS = 8
SWA_BLOCK = 128
EPS = 1e-6

RET_CHUNK = 256

V7X_VMEM_BYTES = 64 * 1024 * 1024
VMEM_LIMIT_BYTES = V7X_VMEM_BYTES - 6 * 1024 * 1024

NEG_BIG = -1e30


def _params(semantics):
    return pltpu.CompilerParams(dimension_semantics=semantics, vmem_limit_bytes=VMEM_LIMIT_BYTES)


def _sigmoid(y):
    return 0.5 * jnp.tanh(0.5 * y) + 0.5


def _mod_kernel(ct_ref, w_ref, b_ref, o_ref, *, batch):
    ct = ct_ref[...]
    ca = ct * _sigmoid(ct)
    w = w_ref[...]
    for b in range(batch):
        o_ref[b:b + 1, :] = jnp.sum(w * ca[:, b:b + 1], axis=0, keepdims=True) + b_ref[...]


def _modulation(c_t, ada_w, ada_b3, batch, tn):
    depth, d, n3 = ada_w.shape
    cols = c_t.shape[1]
    return pl.pallas_call(
        functools.partial(_mod_kernel, batch=batch),
        grid=(depth, n3 // tn),
        in_specs=[
            pl.BlockSpec((d, cols), lambda l, j: (0, 0)),
            pl.BlockSpec((None, d, tn), lambda l, j: (l, 0, j)),
            pl.BlockSpec((None, 1, tn), lambda l, j: (l, 0, j)),
        ],
        out_specs=pl.BlockSpec((None, batch, tn), lambda l, j: (l, 0, j)),
        out_shape=jax.ShapeDtypeStruct((depth, batch, n3), F32),
        compiler_params=_params(("parallel", "parallel")),
        name="adaln_modulation",
    )(c_t, ada_w, ada_b3)


def _modulated_norm(h, nw, scale, shift):
    ms = jnp.mean(h * h, axis=-1, keepdims=True)
    return h * lax.rsqrt(ms + EPS) * (nw * (1.0 + scale)) + shift


def _u_kernel(x_ref, nw_ref, sc_ref, sh_ref, u_ref):
    u_ref[...] = _modulated_norm(x_ref[...], nw_ref[...], sc_ref[...], sh_ref[...]).astype(BF16)


def _first_u(x2, nw, scale, shift, seq, tm):
    m, d = x2.shape
    per_batch = seq // tm
    return pl.pallas_call(
        _u_kernel,
        grid=(m // tm,),
        in_specs=[
            pl.BlockSpec((tm, d), lambda i: (i, 0)),
            pl.BlockSpec((1, d), lambda i: (0, 0)),
            pl.BlockSpec((None, 1, d), lambda i: (i // per_batch, 0, 0)),
            pl.BlockSpec((None, 1, d), lambda i: (i // per_batch, 0, 0)),
        ],
        out_specs=pl.BlockSpec((tm, d), lambda i: (i, 0)),
        out_shape=jax.ShapeDtypeStruct((m, d), BF16),
        compiler_params=_params(("parallel",)),
        name="first_modulated_norm",
    )(x2, nw, scale, shift)


def _inproj_rot_kernel(u_ref, w_ref, cos_ref, sin_ref, o_ref, wbf_ref, *, halves):
    @pl.when(pl.program_id(1) == 0)
    def _():
        wbf_ref[...] = w_ref[...].astype(BF16)

    hd = cos_ref.shape[-1]
    sub = u_ref.shape[0] // halves
    for part in range(halves):
        rows = slice(part * sub, (part + 1) * sub)
        y = jnp.dot(u_ref[rows, :], wbf_ref[...], preferred_element_type=F32)
        cs = cos_ref[rows, :]
        sn = sin_ref[rows, :]
        for h in range(y.shape[1] // hd):
            yh = y[:, h * hd:(h + 1) * hd]
            o_ref[rows, h * hd:(h + 1) * hd] = (yh * cs + pltpu.roll(yh, hd // 2, 1) * sn).astype(BF16)


def _inproj_gen_kernel(*refs, halves, silu_tiles, sigmoid_tiles, n_cast):
    u_ref, w_ref = refs[:2]
    cast_in = refs[2:2 + n_cast]
    o_ref = refs[2 + n_cast]
    cast_out = refs[3 + n_cast:3 + 2 * n_cast]
    wbf_ref = refs[3 + 2 * n_cast]

    @pl.when(pl.program_id(1) == 0)
    def _():
        wbf_ref[...] = w_ref[...].astype(BF16)

    j = pl.program_id(0)
    is_silu = jnp.logical_and(j >= silu_tiles[0], j < silu_tiles[1])
    is_sig = jnp.logical_and(j >= sigmoid_tiles[0], j < sigmoid_tiles[1])
    sub = u_ref.shape[0] // halves

    def body(epilogue):
        for src, dst in zip(cast_in, cast_out):
            dst[...] = src[...].astype(BF16)
        for part in range(halves):
            rows = slice(part * sub, (part + 1) * sub)
            y = jnp.dot(u_ref[rows, :], wbf_ref[...], preferred_element_type=F32)
            o_ref[rows, :] = epilogue(y).astype(BF16)

    @pl.when(is_silu)
    def _():
        body(lambda y: y * _sigmoid(y))

    @pl.when(is_sig)
    def _():
        body(_sigmoid)

    @pl.when(jnp.logical_not(jnp.logical_or(is_silu, is_sig)))
    def _():
        body(lambda y: y)


def _inproj(kernel_fn, u, w_in, layer, col_map, n_tiles, tm, tn, name, extra_specs=(), extra_args=(),
            extra_out_specs=(), extra_out_shapes=()):
    m, d = u.shape
    return pl.pallas_call(
        kernel_fn,
        grid=(n_tiles, m // tm),
        in_specs=[
            pl.BlockSpec((tm, d), lambda j, i: (i, 0)),
            pl.BlockSpec((None, d, tn), lambda j, i: (layer, 0, col_map(j))),
        ] + list(extra_specs),
        out_specs=[pl.BlockSpec((tm, tn), lambda j, i: (i, j))] + list(extra_out_specs),
        out_shape=[jax.ShapeDtypeStruct((m, n_tiles * tn), BF16)] + list(extra_out_shapes),
        scratch_shapes=[pltpu.VMEM((d, tn), BF16)],
        compiler_params=_params(("arbitrary", "arbitrary")),
        name=name,
    )(u, w_in, *extra_args)


def _cast_slab_specs(weights, layer, n_steps, row_tiles):
    n_slabs = 1 << ((n_steps // len(weights)).bit_length() - 1)
    in_specs, out_specs, out_shapes = [], [], []
    for idx, w in enumerate(weights):
        rows, cols = w.shape[1:]
        slab = rows // n_slabs
        assert slab * n_slabs == rows and slab % 16 == 0

        def slab_of(j, i, idx=idx):
            return jnp.clip(j * row_tiles + i - idx * n_slabs, 0, n_slabs - 1)

        in_specs.append(pl.BlockSpec((None, slab, cols), lambda j, i, f=slab_of: (layer, f(j, i), 0)))
        out_specs.append(pl.BlockSpec((slab, cols), lambda j, i, f=slab_of: (f(j, i), 0)))
        out_shapes.append(jax.ShapeDtypeStruct((rows, cols), BF16))
    return in_specs, out_specs, out_shapes


def _retention_heads(qk_ref, v_ref, gnw_ref, o_ref, state_ref, head_list, *, heads, chunk, log_gammas):
    t = qk_ref.shape[0]
    dk = qk_ref.shape[1] // (2 * heads)
    dv = v_ref.shape[1] // heads
    row = lax.broadcasted_iota(jnp.int32, (chunk, chunk), 0)
    col = lax.broadcasted_iota(jnp.int32, (chunk, chunk), 1)
    causal = row >= col
    pos = lax.broadcasted_iota(jnp.int32, (chunk, 1), 0).astype(F32)

    for h in head_list:
        lg = log_gammas[h]
        q_scale = jnp.exp(lg * pos)
        k_scale = jnp.exp(-lg * pos)
        g_chunk = float(np.exp(lg * chunk))
        gnw = gnw_ref[:, h * dv:(h + 1) * dv]
        carry = state_ref[h]
        for c in range(t // chunk):
            rows = slice(c * chunk, (c + 1) * chunk)
            qa = (qk_ref[rows, h * dk:(h + 1) * dk].astype(F32) * q_scale).astype(BF16)
            kb = (qk_ref[rows, (heads + h) * dk:(heads + h + 1) * dk].astype(F32) * k_scale).astype(BF16)
            v = v_ref[rows, h * dv:(h + 1) * dv]
            s = lax.dot_general(qa, kb, (((1,), (1,)), ((), ())), preferred_element_type=F32)
            lhs = jnp.concatenate([jnp.where(causal, s, 0.0).astype(BF16), qa], axis=1)
            rhs = jnp.concatenate([v, carry.astype(BF16)], axis=0)
            o = jnp.dot(lhs, rhs, preferred_element_type=F32)
            kv = lax.dot_general(kb, v, (((0,), (0,)), ((), ())), preferred_element_type=F32)
            carry = (carry + kv) * g_chunk
            mu = jnp.mean(o, axis=-1, keepdims=True)
            dlt = o - mu
            var = jnp.mean(dlt * dlt, axis=-1, keepdims=True)
            o_ref[rows, h * dv:(h + 1) * dv] = (dlt * lax.rsqrt(var + EPS) * gnw).astype(BF16)
        state_ref[h] = carry


def _swa_body(sink_ref, q_ref, kp_ref, kc_ref, vp_ref, vc_ref, g_ref, o_ref, *, kv_heads, group, hd, blk,
              first):
    lanes = 2 * hd
    log2e = float(np.log2(np.e))
    lane = lax.broadcasted_iota(jnp.int32, (1, lanes), 1)
    lo_f = lane < hd
    scale = float(hd) ** -0.5 * log2e
    qmask = (jnp.where(lo_f, scale, 0.0).astype(BF16), jnp.where(lo_f, 0.0, scale).astype(BF16))
    key = lax.broadcasted_iota(jnp.int32, (blk, 2 * blk), 0)
    qry = jnp.bitwise_and(lax.broadcasted_iota(jnp.int32, (blk, 2 * blk), 1), blk - 1)
    upper = key > qry
    head_lo = lax.broadcasted_iota(jnp.int32, (1, 2 * blk), 1) < blk

    def dup(tile_bf16, parity):
        tile = tile_bf16.astype(F32)
        rolled = pltpu.roll(tile, hd, 1)
        keep_lo = lo_f if parity == 0 else jnp.logical_not(lo_f)
        return jnp.where(keep_lo, tile, rolled)

    def key_value_operands(k_tiles, v_tiles):
        ks = [dup(k_tiles[:, (h // 2) * lanes:(h // 2 + 1) * lanes], h % 2).astype(BF16)
              for h in range(kv_heads)]
        vts = [v_tiles[:, t * lanes:(t + 1) * lanes].astype(F32).T.astype(BF16)
               for t in range(kv_heads // 2)]
        return ks, vts

    nt = (((1,), (1,)), ((), ()))
    prev_ops = None if first else key_value_operands(kp_ref[...], vp_ref[...])
    for bi in range(q_ref.shape[0] // blk):
        rows = slice(bi * blk, (bi + 1) * blk)
        cur_ops = key_value_operands(kc_ref[rows, :], vc_ref[rows, :])
        no_prev = prev_ops is None
        for h in range(kv_heads):
            hrows = slice((h % 2) * hd, (h % 2 + 1) * hd)
            k_cur = cur_ops[0][h]
            vt_cur = cur_ops[1][h // 2]
            for gp in range(group // 2):
                qt = (h * group) // 2 + gp
                qcol = slice(qt * lanes, (qt + 1) * lanes)
                qtile = q_ref[rows, qcol]
                qq = jnp.concatenate([qtile * qmask[0], qtile * qmask[1]], axis=0)
                s_cur = lax.dot_general(k_cur, qq, nt, preferred_element_type=F32)
                if no_prev:
                    s = jnp.where(upper, NEG_BIG, s_cur)
                else:
                    s = jnp.where(upper, lax.dot_general(prev_ops[0][h], qq, nt, preferred_element_type=F32),
                                  s_cur)
                hq = h * group + 2 * gp
                sink = jnp.where(head_lo, sink_ref[hq], sink_ref[hq + 1]) * log2e
                mx = jnp.maximum(jnp.max(s, axis=0, keepdims=True), sink)
                p = jnp.exp2(s - mx)
                inv = 1.0 / (jnp.sum(p, axis=0, keepdims=True) + jnp.exp2(sink - mx))
                if no_prev:
                    ot = jnp.dot(vt_cur, p.astype(BF16), preferred_element_type=F32)
                else:
                    ot = (jnp.dot(prev_ops[1][h // 2], jnp.where(upper, p, 0.0).astype(BF16),
                                  preferred_element_type=F32)
                          + jnp.dot(vt_cur, jnp.where(upper, 0.0, p).astype(BF16), preferred_element_type=F32))
                ot = ot[hrows] * inv
                pair_t = jnp.concatenate([ot[:, :blk], ot[:, blk:]], axis=0)
                o_ref[rows, qcol] = (pair_t.T * g_ref[rows, qcol].astype(F32)).astype(BF16)
        prev_ops = cur_ops


def _swa_kernel(*refs, **kw):
    is_first = pl.program_id(1) == 0

    @pl.when(is_first)
    def _():
        _swa_body(*refs, first=True, **kw)

    @pl.when(jnp.logical_not(is_first))
    def _():
        _swa_body(*refs, first=False, **kw)


def _swa(proj, sinks_l, seq, q_w, kv_w, q_blk, k_blk, v_blk, g_blk, blocks_per_step):
    m = proj.shape[0]
    t = blocks_per_step * SWA_BLOCK
    steps = seq // t
    kv_heads = SWA_KV_HEADS
    group = q_w // SWA_HEAD_DIM // kv_heads
    cur = lambda b, n: b * steps + n
    prev = lambda b, n: (b * steps + n) * blocks_per_step - jnp.minimum(n, 1)
    return pl.pallas_call(
        functools.partial(_swa_kernel, kv_heads=kv_heads, group=group, hd=SWA_HEAD_DIM, blk=SWA_BLOCK),
        grid=(m // seq, steps),
        in_specs=[
            pl.BlockSpec(memory_space=pltpu.SMEM),
            pl.BlockSpec((t, q_w), lambda b, n: (cur(b, n), q_blk)),
            pl.BlockSpec((SWA_BLOCK, kv_w), lambda b, n: (prev(b, n), k_blk)),
            pl.BlockSpec((t, kv_w), lambda b, n: (cur(b, n), k_blk)),
            pl.BlockSpec((SWA_BLOCK, kv_w), lambda b, n: (prev(b, n), v_blk)),
            pl.BlockSpec((t, kv_w), lambda b, n: (cur(b, n), v_blk)),
            pl.BlockSpec((t, q_w), lambda b, n: (cur(b, n), g_blk)),
        ],
        out_specs=pl.BlockSpec((t, q_w), lambda b, n: (cur(b, n), 0)),
        out_shape=jax.ShapeDtypeStruct((m, q_w), BF16),
        compiler_params=_params(("parallel", "arbitrary")),
        name="swa",
    )(sinks_l, proj, proj, proj, proj, proj, proj)


def _outproj_kernel(*refs, last, ret_kw, tiles_per_seq):
    (qk0_ref, rv0_ref, qk_ref, rv_ref, gnw_ref, rg_ref, a_ref, sg_ref, h_ref, gate_ref,
     wr_ref, ws_ref, wo_ref) = refs[:13]
    if last:
        fw_ref, out_ref, rbuf_ref, state_ref = refs[13:]
    else:
        nw_ref, sc_ref, sh_ref, hn_ref, u_ref, rbuf_ref, state_ref = refs[13:]
    d = h_ref.shape[1]
    i = pl.program_id(0)
    heads = ret_kw["heads"]
    slot = lax.rem(i, 2)

    @pl.when(i == 0)
    def _():
        state_ref[...] = jnp.zeros_like(state_ref)
        _retention_heads(qk0_ref, rv0_ref, gnw_ref, rbuf_ref.at[0], state_ref, range(heads), **ret_kw)

    @pl.when(lax.rem(i + 1, tiles_per_seq) == 0)
    def _():
        state_ref[...] = jnp.zeros_like(state_ref)

    nxt_buf = rbuf_ref.at[1 - slot]
    y1 = jnp.dot(rbuf_ref[slot] * rg_ref[...], wr_ref[...], preferred_element_type=F32)
    _retention_heads(qk_ref, rv_ref, gnw_ref, nxt_buf, state_ref, range(0, heads // 2), **ret_kw)
    y2 = jnp.dot(a_ref[...], ws_ref[...], preferred_element_type=F32)
    _retention_heads(qk_ref, rv_ref, gnw_ref, nxt_buf, state_ref, range(heads // 2, heads), **ret_kw)
    merged = sg_ref[:, :d].astype(F32) * y1 + sg_ref[:, d:].astype(F32) * y2
    z = jnp.dot(merged.astype(BF16), wo_ref[...], preferred_element_type=F32)
    hn = h_ref[...] + gate_ref[...] * z
    if last:
        ms = jnp.mean(hn * hn, axis=-1, keepdims=True)
        out_ref[...] = hn * lax.rsqrt(ms + EPS) * fw_ref[...]
    else:
        hn_ref[...] = hn
        u_ref[...] = _modulated_norm(hn, nw_ref[...], sc_ref[...], sh_ref[...]).astype(BF16)


def _outproj(rqk, gn_w3, layer, a, proj, rv_blk, rg_blk, sig_blk, h, gate, wr, ws, wo, seq, tm,
             nxt=None, final_w=None):
    m, d = h.shape
    qk_w = rqk.shape[1]
    v_w = gn_w3.shape[-1]
    n = m // tm
    per_batch = seq // tm
    assert tm == RET_CHUNK
    last = nxt is None
    row = lambda i: (i, 0)
    nxt_row = lambda i: jnp.minimum(i + 1, n - 1)
    vec = pl.BlockSpec((None, 1, d), lambda i: (i // per_batch, 0, 0))
    once = lambda shape, col: pl.BlockSpec(shape, lambda i: (0, col), pipeline_mode=pl.Buffered(1))
    in_specs = [
        once((tm, qk_w), 0), once((tm, v_w), rv_blk),
        pl.BlockSpec((tm, qk_w), lambda i: (nxt_row(i), 0)),
        pl.BlockSpec((tm, v_w), lambda i: (nxt_row(i), rv_blk)),
        pl.BlockSpec((None, 1, v_w), lambda i: (layer, 0, 0)),
        pl.BlockSpec((tm, v_w), lambda i: (i, rg_blk)),
        pl.BlockSpec((tm, d), row), pl.BlockSpec((tm, 2 * d), lambda i: (i, sig_blk)),
        pl.BlockSpec((tm, d), row), vec, once((v_w, d), 0), once((d, d), 0), once((d, d), 0),
    ]
    args = [rqk, proj, rqk, proj, gn_w3, proj, a, proj, h, gate, wr, ws, wo]
    log_gammas = tuple(float(np.log1p(-np.exp2(-5.0 - hh))) for hh in range(RET_HEADS))
    ret_kw = dict(heads=RET_HEADS, chunk=RET_CHUNK, log_gammas=log_gammas)
    scratch = [pltpu.VMEM((2, tm, v_w), BF16),
               pltpu.VMEM((RET_HEADS, qk_w // (2 * RET_HEADS), v_w // RET_HEADS), F32)]
    if last:
        in_specs.append(pl.BlockSpec((1, d), lambda i: (0, 0)))
        args.append(final_w)
        out_specs = pl.BlockSpec((tm, d), row)
        out_shape = jax.ShapeDtypeStruct((m, d), F32)
    else:
        nw, sc, sh = nxt
        in_specs += [pl.BlockSpec((1, d), lambda i: (0, 0)), vec, vec]
        args += [nw, sc, sh]
        out_specs = (pl.BlockSpec((tm, d), row), pl.BlockSpec((tm, d), row))
        out_shape = (jax.ShapeDtypeStruct((m, d), F32), jax.ShapeDtypeStruct((m, d), BF16))
    return pl.pallas_call(
        functools.partial(_outproj_kernel, last=last, ret_kw=ret_kw, tiles_per_seq=per_batch),
        grid=(n,),
        in_specs=in_specs,
        out_specs=out_specs,
        out_shape=out_shape,
        scratch_shapes=scratch,
        compiler_params=_params(("arbitrary",)),
        name="outproj_last" if last else "outproj",
    )(*args)


def kernel(x, c, norm_w, ada_w, ada_b, w_in, ret_gn_w, attn_sinks, w_ret_o, w_swa_o, w_out, final_norm_w):
    b, s, d = x.shape
    depth = norm_w.shape[0]
    m = b * s
    ret_v_w = ret_gn_w.shape[1]
    ret_qk_w = ret_v_w // 2
    swa_q_w = d
    swa_kv_w = SWA_KV_HEADS * SWA_HEAD_DIM
    tn = 1024
    tm = 2048
    assert w_in.shape[2] == 2 * ret_qk_w + 2 * ret_v_w + 2 * swa_q_w + 2 * swa_kv_w + 2 * d
    assert ret_qk_w == tn and 2 * swa_kv_w == tn and ret_v_w == 2 * tn and d == 2 * tn and s % tm == 0

    c_t = jnp.pad(c.T, ((0, 0), (0, 8 - b)))
    mod = _modulation(c_t, ada_w, ada_b.reshape(depth, 1, 3 * d), b, tn=1536)
    shift = mod[:, :, :d].reshape(depth, b, 1, d)
    scale = mod[:, :, d:2 * d].reshape(depth, b, 1, d)
    gate = mod[:, :, 2 * d:].reshape(depth, b, 1, d)

    half = ret_qk_w // RET_HEADS // 2
    inv = (1.0 / (RET_ROT_BASE ** np.linspace(0.0, 1.0, half, dtype=np.float32))).astype(np.float32)
    ang = np.arange(s, dtype=np.float32)[:, None] * inv[None, :]
    cos2 = np.concatenate([np.cos(ang), np.cos(ang)], axis=-1)
    sin2 = np.concatenate([-np.sin(ang), np.sin(ang)], axis=-1)
    k_scale = np.float32(float(2 * half) ** -0.5)
    cos_tab = jnp.asarray(np.stack([cos2, cos2 * k_scale]), dtype=F32)
    sin_tab = jnp.asarray(np.stack([sin2, sin2 * k_scale]), dtype=F32)

    gn_w3 = ret_gn_w.reshape(depth, 1, ret_v_w)
    nw2 = norm_w.reshape(depth, 1, d)

    h = x.reshape(m, d)
    u = _first_u(h, nw2[0], scale[0], shift[0], s, tm=1024)

    gen_map = lambda j: jnp.where(j < 2, j + 2, jnp.where(j < 4, j + 4, jnp.where(j < 6, j,
                                  jnp.where(j < 12, j + 3, 8))))
    gen_tiles = 13
    off_rv, off_sq, off_rg, off_sg, off_sig, off_sk = (i * tn for i in (0, 2, 4, 6, 8, 12))
    off_sv = off_sk + swa_kv_w
    out_weights = (w_ret_o, w_swa_o, w_out)
    gen_kernel = functools.partial(_inproj_gen_kernel, halves=2, silu_tiles=(4, 8), sigmoid_tiles=(8, 12),
                                   n_cast=len(out_weights))
    rot_kernel = functools.partial(_inproj_rot_kernel, halves=2)
    per_batch = s // tm
    hd2 = cos_tab.shape[-1]
    tab_spec = pl.BlockSpec((None, tm, hd2), lambda j, i: (j, i % per_batch, 0))

    for l in range(depth):
        (rqk,) = _inproj(rot_kernel, u, w_in, l, lambda j: j, 2, tm, tn, "inproj_rot",
                         extra_specs=(tab_spec, tab_spec), extra_args=(cos_tab, sin_tab))
        cast_in, cast_out, cast_shapes = _cast_slab_specs(out_weights, l, gen_tiles * (m // tm), m // tm)
        proj, wr_bf, ws_bf, wo_bf = _inproj(gen_kernel, u, w_in, l, gen_map, gen_tiles, tm, tn, "inproj_gen",
                                            extra_specs=cast_in, extra_args=out_weights,
                                            extra_out_specs=cast_out, extra_out_shapes=cast_shapes)
        a = _swa(proj, attn_sinks[l], s, swa_q_w, swa_kv_w, q_blk=off_sq // swa_q_w, k_blk=off_sk // swa_kv_w,
                 v_blk=off_sv // swa_kv_w, g_blk=off_sg // swa_q_w, blocks_per_step=4)
        out_args = (rqk, gn_w3, l, a, proj, off_rv // ret_v_w, off_rg // ret_v_w, off_sig // (2 * d),
                    h, gate[l], wr_bf, ws_bf, wo_bf, s)
        if l + 1 < depth:
            h, u = _outproj(*out_args, tm=256, nxt=(nw2[l + 1], scale[l + 1], shift[l + 1]))
        else:
            h = _outproj(*out_args, tm=256, final_w=final_norm_w.reshape(1, d))
    return h.reshape(b, s, d)
```

```python
import functools

import numpy as np
import jax
import jax.numpy as jnp
from jax import lax
from jax.experimental import pallas as pl
from jax.experimental.pallas import tpu as pltpu

F32 = jnp.float32
BF16 = jnp.bfloat16

RET_HEADS = 8
RET_ROT_BASE = 10000.0
SWA_HEAD_DIM = 64
SWA_KV_HEADS = 8
SWA_BLOCK = 128
EPS = 1e-6

RET_CHUNK = 256

V7X_VMEM_BYTES = 64 * 1024 * 1024
VMEM_LIMIT_BYTES = V7X_VMEM_BYTES - 6 * 1024 * 1024

NEG_BIG = -1e30


def _params(semantics):
    return pltpu.CompilerParams(dimension_semantics=semantics, vmem_limit_bytes=VMEM_LIMIT_BYTES)


def _sigmoid(y):
    return 0.5 * jnp.tanh(0.5 * y) + 0.5


def _mod_kernel(ct_ref, w_ref, b_ref, o_ref, *, batch):
    ct = ct_ref[...]
    ca = ct * _sigmoid(ct)
    w = w_ref[...]
    for b in range(batch):
        o_ref[b:b + 1, :] = jnp.sum(w * ca[:, b:b + 1], axis=0, keepdims=True) + b_ref[...]


def _modulation(c_t, ada_w, ada_b3, batch, tn):
    depth, d, n3 = ada_w.shape
    cols = c_t.shape[1]
    return pl.pallas_call(
        functools.partial(_mod_kernel, batch=batch),
        grid=(depth, n3 // tn),
        in_specs=[
            pl.BlockSpec((d, cols), lambda l, j: (0, 0)),
            pl.BlockSpec((None, d, tn), lambda l, j: (l, 0, j)),
            pl.BlockSpec((None, 1, tn), lambda l, j: (l, 0, j)),
        ],
        out_specs=pl.BlockSpec((None, batch, tn), lambda l, j: (l, 0, j)),
        out_shape=jax.ShapeDtypeStruct((depth, batch, n3), F32),
        compiler_params=_params(("parallel", "parallel")),
        name="adaln_modulation",
    )(c_t, ada_w, ada_b3)


def _modulated_norm(h, nw, scale, shift):
    ms = jnp.mean(h * h, axis=-1, keepdims=True)
    return h * lax.rsqrt(ms + EPS) * (nw * (1.0 + scale)) + shift


def _u_kernel(x_ref, nw_ref, sc_ref, sh_ref, u_ref):
    u_ref[...] = _modulated_norm(x_ref[...], nw_ref[...], sc_ref[...], sh_ref[...]).astype(BF16)


def _first_u(x2, nw, scale, shift, seq, tm):
    m, d = x2.shape
    per_batch = seq // tm
    return pl.pallas_call(
        _u_kernel,
        grid=(m // tm,),
        in_specs=[
            pl.BlockSpec((tm, d), lambda i: (i, 0)),
            pl.BlockSpec((1, d), lambda i: (0, 0)),
            pl.BlockSpec((None, 1, d), lambda i: (i // per_batch, 0, 0)),
            pl.BlockSpec((None, 1, d), lambda i: (i // per_batch, 0, 0)),
        ],
        out_specs=pl.BlockSpec((tm, d), lambda i: (i, 0)),
        out_shape=jax.ShapeDtypeStruct((m, d), BF16),
        compiler_params=_params(("parallel",)),
        name="first_modulated_norm",
    )(x2, nw, scale, shift)


def _inproj_rot_kernel(u_ref, w_ref, cos_ref, sin_ref, o_ref, wbf_ref, *, halves):
    @pl.when(pl.program_id(1) == 0)
    def _():
        wbf_ref[...] = w_ref[...].astype(BF16)

    hd = cos_ref.shape[-1]
    sub = u_ref.shape[0] // halves
    for part in range(halves):
        rows = slice(part * sub, (part + 1) * sub)
        y = jnp.dot(u_ref[rows, :], wbf_ref[...], preferred_element_type=F32)
        cs = cos_ref[rows, :]
        sn = sin_ref[rows, :]
        for h in range(y.shape[1] // hd):
            yh = y[:, h * hd:(h + 1) * hd]
            o_ref[rows, h * hd:(h + 1) * hd] = (yh * cs + pltpu.roll(yh, hd // 2, 1) * sn).astype(BF16)


def _inproj_gen_kernel(*refs, halves, silu_tiles, sigmoid_tiles, n_cast):
    u_ref, w_ref = refs[:2]
    cast_in = refs[2:2 + n_cast]
    o_ref = refs[2 + n_cast]
    cast_out = refs[3 + n_cast:3 + 2 * n_cast]
    wbf_ref = refs[3 + 2 * n_cast]

    @pl.when(pl.program_id(1) == 0)
    def _():
        wbf_ref[...] = w_ref[...].astype(BF16)

    j = pl.program_id(0)
    is_silu = jnp.logical_and(j >= silu_tiles[0], j < silu_tiles[1])
    is_sig = jnp.logical_and(j >= sigmoid_tiles[0], j < sigmoid_tiles[1])
    sub = u_ref.shape[0] // halves

    def body(epilogue):
        for src, dst in zip(cast_in, cast_out):
            dst[...] = src[...].astype(BF16)
        for part in range(halves):
            rows = slice(part * sub, (part + 1) * sub)
            y = jnp.dot(u_ref[rows, :], wbf_ref[...], preferred_element_type=F32)
            o_ref[rows, :] = epilogue(y).astype(BF16)

    @pl.when(is_silu)
    def _():
        body(lambda y: y * _sigmoid(y))

    @pl.when(is_sig)
    def _():
        body(_sigmoid)

    @pl.when(jnp.logical_not(jnp.logical_or(is_silu, is_sig)))
    def _():
        body(lambda y: y)


def _inproj(kernel_fn, u, w_in, layer, col_map, n_tiles, tm, tn, name, extra_specs=(), extra_args=(),
            extra_out_specs=(), extra_out_shapes=()):
    m, d = u.shape
    return pl.pallas_call(
        kernel_fn,
        grid=(n_tiles, m // tm),
        in_specs=[
            pl.BlockSpec((tm, d), lambda j, i: (i, 0)),
            pl.BlockSpec((None, d, tn), lambda j, i: (layer, 0, col_map(j))),
        ] + list(extra_specs),
        out_specs=[pl.BlockSpec((tm, tn), lambda j, i: (i, j))] + list(extra_out_specs),
        out_shape=[jax.ShapeDtypeStruct((m, n_tiles * tn), BF16)] + list(extra_out_shapes),
        scratch_shapes=[pltpu.VMEM((d, tn), BF16)],
        compiler_params=_params(("arbitrary", "arbitrary")),
        name=name,
    )(u, w_in, *extra_args)


def _cast_slab_specs(weights, layer, n_steps, row_tiles):
    n_slabs = 1 << ((n_steps // len(weights)).bit_length() - 1)
    in_specs, out_specs, out_shapes = [], [], []
    for idx, w in enumerate(weights):
        rows, cols = w.shape[1:]
        slab = rows // n_slabs
        assert slab * n_slabs == rows and slab % 16 == 0

        def slab_of(j, i, idx=idx):
            return jnp.clip(j * row_tiles + i - idx * n_slabs, 0, n_slabs - 1)

        in_specs.append(pl.BlockSpec((None, slab, cols), lambda j, i, f=slab_of: (layer, f(j, i), 0)))
        out_specs.append(pl.BlockSpec((slab, cols), lambda j, i, f=slab_of: (f(j, i), 0)))
        out_shapes.append(jax.ShapeDtypeStruct((rows, cols), BF16))
    return in_specs, out_specs, out_shapes


def _retention_heads(qk_ref, v_ref, gnw_ref, o_ref, state_ref, head_list, *, heads, chunk, log_gammas):
    t = qk_ref.shape[0]
    dk = qk_ref.shape[1] // (2 * heads)
    dv = v_ref.shape[1] // heads
    row = lax.broadcasted_iota(jnp.int32, (chunk, chunk), 0)
    col = lax.broadcasted_iota(jnp.int32, (chunk, chunk), 1)
    causal = row >= col
    pos = lax.broadcasted_iota(jnp.int32, (chunk, 1), 0).astype(F32)

    for h in head_list:
        lg = log_gammas[h]
        q_scale = jnp.exp(lg * pos)
        k_scale = jnp.exp(-lg * pos)
        g_chunk = float(np.exp(lg * chunk))
        gnw = gnw_ref[:, h * dv:(h + 1) * dv]
        carry = state_ref[h]
        for c in range(t // chunk):
            rows = slice(c * chunk, (c + 1) * chunk)
            qa = (qk_ref[rows, h * dk:(h + 1) * dk].astype(F32) * q_scale).astype(BF16)
            kb = (qk_ref[rows, (heads + h) * dk:(heads + h + 1) * dk].astype(F32) * k_scale).astype(BF16)
            v = v_ref[rows, h * dv:(h + 1) * dv]
            s = lax.dot_general(qa, kb, (((1,), (1,)), ((), ())), preferred_element_type=F32)
            lhs = jnp.concatenate([jnp.where(causal, s, 0.0).astype(BF16), qa], axis=1)
            rhs = jnp.concatenate([v, carry.astype(BF16)], axis=0)
            o = jnp.dot(lhs, rhs, preferred_element_type=F32)
            kv = lax.dot_general(kb, v, (((0,), (0,)), ((), ())), preferred_element_type=F32)
            carry = (carry + kv) * g_chunk
            mu = jnp.mean(o, axis=-1, keepdims=True)
            dlt = o - mu
            var = jnp.mean(dlt * dlt, axis=-1, keepdims=True)
            o_ref[rows, h * dv:(h + 1) * dv] = (dlt * lax.rsqrt(var + EPS) * gnw).astype(BF16)
        state_ref[h] = carry


def _swa_body(sink_ref, q_ref, kp_ref, kc_ref, vp_ref, vc_ref, g_ref, o_ref, *, kv_heads, group, hd, blk,
              first):
    lanes = 2 * hd
    log2e = float(np.log2(np.e))
    lane = lax.broadcasted_iota(jnp.int32, (1, lanes), 1)
    lo_f = lane < hd
    scale = float(hd) ** -0.5 * log2e
    qmask = (jnp.where(lo_f, scale, 0.0).astype(BF16), jnp.where(lo_f, 0.0, scale).astype(BF16))
    key = lax.broadcasted_iota(jnp.int32, (blk, 2 * blk), 0)
    qry = jnp.bitwise_and(lax.broadcasted_iota(jnp.int32, (blk, 2 * blk), 1), blk - 1)
    upper = key > qry
    head_lo = lax.broadcasted_iota(jnp.int32, (1, 2 * blk), 1) < blk

    def dup(tile_bf16, parity):
        tile = tile_bf16.astype(F32)
        rolled = pltpu.roll(tile, hd, 1)
        keep_lo = lo_f if parity == 0 else jnp.logical_not(lo_f)
        return jnp.where(keep_lo, tile, rolled)

    def key_value_operands(k_tiles, v_tiles):
        ks = [dup(k_tiles[:, (h // 2) * lanes:(h // 2 + 1) * lanes], h % 2).astype(BF16)
              for h in range(kv_heads)]
        vts = [v_tiles[:, t * lanes:(t + 1) * lanes].astype(F32).T.astype(BF16)
               for t in range(kv_heads // 2)]
        return ks, vts

    nt = (((1,), (1,)), ((), ()))
    prev_ops = None if first else key_value_operands(kp_ref[...], vp_ref[...])
    for bi in range(q_ref.shape[0] // blk):
        rows = slice(bi * blk, (bi + 1) * blk)
        cur_ops = key_value_operands(kc_ref[rows, :], vc_ref[rows, :])
        no_prev = prev_ops is None
        for h in range(kv_heads):
            hrows = slice((h % 2) * hd, (h % 2 + 1) * hd)
            k_cur = cur_ops[0][h]
            vt_cur = cur_ops[1][h // 2]
            for gp in range(group // 2):
                qt = (h * group) // 2 + gp
                qcol = slice(qt * lanes, (qt + 1) * lanes)
                qtile = q_ref[rows, qcol]
                qq = jnp.concatenate([qtile * qmask[0], qtile * qmask[1]], axis=0)
                s_cur = lax.dot_general(k_cur, qq, nt, preferred_element_type=F32)
                if no_prev:
                    s = jnp.where(upper, NEG_BIG, s_cur)
                else:
                    s = jnp.where(upper, lax.dot_general(prev_ops[0][h], qq, nt, preferred_element_type=F32),
                                  s_cur)
                hq = h * group + 2 * gp
                sink = jnp.where(head_lo, sink_ref[hq], sink_ref[hq + 1]) * log2e
                mx = jnp.maximum(jnp.max(s, axis=0, keepdims=True), sink)
                p = jnp.exp2(s - mx)
                inv = 1.0 / (jnp.sum(p, axis=0, keepdims=True) + jnp.exp2(sink - mx))
                if no_prev:
                    ot = jnp.dot(vt_cur, p.astype(BF16), preferred_element_type=F32)
                else:
                    ot = (jnp.dot(prev_ops[1][h // 2], jnp.where(upper, p, 0.0).astype(BF16),
                                  preferred_element_type=F32)
                          + jnp.dot(vt_cur, jnp.where(upper, 0.0, p).astype(BF16), preferred_element_type=F32))
                ot = ot[hrows] * inv
                pair_t = jnp.concatenate([ot[:, :blk], ot[:, blk:]], axis=0)
                o_ref[rows, qcol] = (pair_t.T * g_ref[rows, qcol].astype(F32)).astype(BF16)
        prev_ops = cur_ops


def _swa_kernel(*refs, **kw):
    is_first = pl.program_id(1) == 0

    @pl.when(is_first)
    def _():
        _swa_body(*refs, first=True, **kw)

    @pl.when(jnp.logical_not(is_first))
    def _():
        _swa_body(*refs, first=False, **kw)


def _swa(proj, sinks_l, seq, q_w, kv_w, q_blk, k_blk, v_blk, g_blk, blocks_per_step):
    m = proj.shape[0]
    t = blocks_per_step * SWA_BLOCK
    steps = seq // t
    kv_heads = SWA_KV_HEADS
    group = q_w // SWA_HEAD_DIM // kv_heads
    cur = lambda b, n: b * steps + n
    prev = lambda b, n: (b * steps + n) * blocks_per_step - jnp.minimum(n, 1)
    return pl.pallas_call(
        functools.partial(_swa_kernel, kv_heads=kv_heads, group=group, hd=SWA_HEAD_DIM, blk=SWA_BLOCK),
        grid=(m // seq, steps),
        in_specs=[
            pl.BlockSpec(memory_space=pltpu.SMEM),
            pl.BlockSpec((t, q_w), lambda b, n: (cur(b, n), q_blk)),
            pl.BlockSpec((SWA_BLOCK, kv_w), lambda b, n: (prev(b, n), k_blk)),
            pl.BlockSpec((t, kv_w), lambda b, n: (cur(b, n), k_blk)),
            pl.BlockSpec((SWA_BLOCK, kv_w), lambda b, n: (prev(b, n), v_blk)),
            pl.BlockSpec((t, kv_w), lambda b, n: (cur(b, n), v_blk)),
            pl.BlockSpec((t, q_w), lambda b, n: (cur(b, n), g_blk)),
        ],
        out_specs=pl.BlockSpec((t, q_w), lambda b, n: (cur(b, n), 0)),
        out_shape=jax.ShapeDtypeStruct((m, q_w), BF16),
        compiler_params=_params(("parallel", "arbitrary")),
        name="swa",
    )(sinks_l, proj, proj, proj, proj, proj, proj)


def _outproj_kernel(*refs, last, ret_kw, tiles_per_seq):
    (qk0_ref, rv0_ref, qk_ref, rv_ref, gnw_ref, rg_ref, a_ref, sg_ref, h_ref, gate_ref,
     wr_ref, ws_ref, wo_ref) = refs[:13]
    if last:
        fw_ref, out_ref, rbuf_ref, state_ref = refs[13:]
    else:
        nw_ref, sc_ref, sh_ref, hn_ref, u_ref, rbuf_ref, state_ref = refs[13:]
    d = h_ref.shape[1]
    i = pl.program_id(0)
    heads = ret_kw["heads"]
    slot = lax.rem(i, 2)

    @pl.when(i == 0)
    def _():
        state_ref[...] = jnp.zeros_like(state_ref)
        _retention_heads(qk0_ref, rv0_ref, gnw_ref, rbuf_ref.at[0], state_ref, range(heads), **ret_kw)

    @pl.when(lax.rem(i + 1, tiles_per_seq) == 0)
    def _():
        state_ref[...] = jnp.zeros_like(state_ref)

    nxt_buf = rbuf_ref.at[1 - slot]
    y1 = jnp.dot(rbuf_ref[slot] * rg_ref[...], wr_ref[...], preferred_element_type=F32)
    _retention_heads(qk_ref, rv_ref, gnw_ref, nxt_buf, state_ref, range(0, heads // 2), **ret_kw)
    y2 = jnp.dot(a_ref[...], ws_ref[...], preferred_element_type=F32)
    _retention_heads(qk_ref, rv_ref, gnw_ref, nxt_buf, state_ref, range(heads // 2, heads), **ret_kw)
    merged = sg_ref[:, :d].astype(F32) * y1 + sg_ref[:, d:].astype(F32) * y2
    z = jnp.dot(merged.astype(BF16), wo_ref[...], preferred_element_type=F32)
    hn = h_ref[...] + gate_ref[...] * z
    if last:
        ms = jnp.mean(hn * hn, axis=-1, keepdims=True)
        out_ref[...] = hn * lax.rsqrt(ms + EPS) * fw_ref[...]
    else:
        hn_ref[...] = hn
        u_ref[...] = _modulated_norm(hn, nw_ref[...], sc_ref[...], sh_ref[...]).astype(BF16)


def _outproj(rqk, gn_w3, layer, a, proj, rv_blk, rg_blk, sig_blk, h, gate, wr, ws, wo, seq, tm,
             nxt=None, final_w=None):
    m, d = h.shape
    qk_w = rqk.shape[1]
    v_w = gn_w3.shape[-1]
    n = m // tm
    per_batch = seq // tm
    assert tm == RET_CHUNK
    last = nxt is None
    row = lambda i: (i, 0)
    nxt_row = lambda i: jnp.minimum(i + 1, n - 1)
    vec = pl.BlockSpec((None, 1, d), lambda i: (i // per_batch, 0, 0))
    once = lambda shape, col: pl.BlockSpec(shape, lambda i: (0, col), pipeline_mode=pl.Buffered(1))
    in_specs = [
        once((tm, qk_w), 0), once((tm, v_w), rv_blk),
        pl.BlockSpec((tm, qk_w), lambda i: (nxt_row(i), 0)),
        pl.BlockSpec((tm, v_w), lambda i: (nxt_row(i), rv_blk)),
        pl.BlockSpec((None, 1, v_w), lambda i: (layer, 0, 0)),
        pl.BlockSpec((tm, v_w), lambda i: (i, rg_blk)),
        pl.BlockSpec((tm, d), row), pl.BlockSpec((tm, 2 * d), lambda i: (i, sig_blk)),
        pl.BlockSpec((tm, d), row), vec, once((v_w, d), 0), once((d, d), 0), once((d, d), 0),
    ]
    args = [rqk, proj, rqk, proj, gn_w3, proj, a, proj, h, gate, wr, ws, wo]
    log_gammas = tuple(float(np.log1p(-np.exp2(-5.0 - hh))) for hh in range(RET_HEADS))
    ret_kw = dict(heads=RET_HEADS, chunk=RET_CHUNK, log_gammas=log_gammas)
    scratch = [pltpu.VMEM((2, tm, v_w), BF16),
               pltpu.VMEM((RET_HEADS, qk_w // (2 * RET_HEADS), v_w // RET_HEADS), F32)]
    if last:
        in_specs.append(pl.BlockSpec((1, d), lambda i: (0, 0)))
        args.append(final_w)
        out_specs = pl.BlockSpec((tm, d), row)
        out_shape = jax.ShapeDtypeStruct((m, d), F32)
    else:
        nw, sc, sh = nxt
        in_specs += [pl.BlockSpec((1, d), lambda i: (0, 0)), vec, vec]
        args += [nw, sc, sh]
        out_specs = (pl.BlockSpec((tm, d), row), pl.BlockSpec((tm, d), row))
        out_shape = (jax.ShapeDtypeStruct((m, d), F32), jax.ShapeDtypeStruct((m, d), BF16))
    return pl.pallas_call(
        functools.partial(_outproj_kernel, last=last, ret_kw=ret_kw, tiles_per_seq=per_batch),
        grid=(n,),
        in_specs=in_specs,
        out_specs=out_specs,
        out_shape=out_shape,
        scratch_shapes=scratch,
        compiler_params=_params(("arbitrary",)),
        name="outproj_last" if last else "outproj",
    )(*args)


def kernel(x, c, norm_w, ada_w, ada_b, w_in, ret_gn_w, attn_sinks, w_ret_o, w_swa_o, w_out, final_norm_w):
    b, s, d = x.shape
    depth = norm_w.shape[0]
    m = b * s
    ret_v_w = ret_gn_w.shape[1]
    ret_qk_w = ret_v_w // 2
    swa_q_w = d
    swa_kv_w = SWA_KV_HEADS * SWA_HEAD_DIM
    tn = 1024
    tm = 2048
    assert w_in.shape[2] == 2 * ret_qk_w + 2 * ret_v_w + 2 * swa_q_w + 2 * swa_kv_w + 2 * d
    assert ret_qk_w == tn and 2 * swa_kv_w == tn and ret_v_w == 2 * tn and d == 2 * tn and s % tm == 0

    c_t = jnp.pad(c.T, ((0, 0), (0, 8 - b)))
    mod = _modulation(c_t, ada_w, ada_b.reshape(depth, 1, 3 * d), b, tn=1536)
    shift = mod[:, :, :d].reshape(depth, b, 1, d)
    scale = mod[:, :, d:2 * d].reshape(depth, b, 1, d)
    gate = mod[:, :, 2 * d:].reshape(depth, b, 1, d)

    half = ret_qk_w // RET_HEADS // 2
    inv = (1.0 / (RET_ROT_BASE ** np.linspace(0.0, 1.0, half, dtype=np.float32))).astype(np.float32)
    ang = np.arange(s, dtype=np.float32)[:, None] * inv[None, :]
    cos2 = np.concatenate([np.cos(ang), np.cos(ang)], axis=-1)
    sin2 = np.concatenate([-np.sin(ang), np.sin(ang)], axis=-1)
    k_scale = np.float32(float(2 * half) ** -0.5)
    cos_tab = jnp.asarray(np.stack([cos2, cos2 * k_scale]), dtype=F32)
    sin_tab = jnp.asarray(np.stack([sin2, sin2 * k_scale]), dtype=F32)

    gn_w3 = ret_gn_w.reshape(depth, 1, ret_v_w)
    nw2 = norm_w.reshape(depth, 1, d)

    h = x.reshape(m, d)
    u = _first_u(h, nw2[0], scale[0], shift[0], s, tm=1024)

    gen_map = lambda j: jnp.where(j < 2, j + 2, jnp.where(j < 4, j + 4, jnp.where(j < 6, j,
                                  jnp.where(j < 12, j + 3, 8))))
    gen_tiles = 13
    off_rv, off_sq, off_rg, off_sg, off_sig, off_sk = (i * tn for i in (0, 2, 4, 6, 8, 12))
    off_sv = off_sk + swa_kv_w
    out_weights = (w_ret_o, w_swa_o, w_out)
    gen_kernel = functools.partial(_inproj_gen_kernel, halves=4, silu_tiles=(4, 8), sigmoid_tiles=(8, 12),
                                   n_cast=len(out_weights))
    rot_kernel = functools.partial(_inproj_rot_kernel, halves=4)
    per_batch = s // tm
    hd2 = cos_tab.shape[-1]
    tab_spec = pl.BlockSpec((None, tm, hd2), lambda j, i: (j, i % per_batch, 0))

    for l in range(depth):
        (rqk,) = _inproj(rot_kernel, u, w_in, l, lambda j: j, 2, tm, tn, "inproj_rot",
                         extra_specs=(tab_spec, tab_spec), extra_args=(cos_tab, sin_tab))
        cast_in, cast_out, cast_shapes = _cast_slab_specs(out_weights, l, gen_tiles * (m // tm), m // tm)
        proj, wr_bf, ws_bf, wo_bf = _inproj(gen_kernel, u, w_in, l, gen_map, gen_tiles, tm, tn, "inproj_gen",
                                            extra_specs=cast_in, extra_args=out_weights,
                                            extra_out_specs=cast_out, extra_out_shapes=cast_shapes)
        a = _swa(proj, attn_sinks[l], s, swa_q_w, swa_kv_w, q_blk=off_sq // swa_q_w, k_blk=off_sk // swa_kv_w,
                 v_blk=off_sv // swa_kv_w, g_blk=off_sg // swa_q_w, blocks_per_step=4)
        out_args = (rqk, gn_w3, l, a, proj, off_rv // ret_v_w, off_rg // ret_v_w, off_sig // (2 * d),
                    h, gate[l], wr_bf, ws_bf, wo_bf, s)
        if l + 1 < depth:
            h, u = _outproj(*out_args, tm=256, nxt=(nw2[l + 1], scale[l + 1], shift[l + 1]))
        else:
            h = _outproj(*out_args, tm=256, final_w=final_norm_w.reshape(1, d))
    return h.reshape(b, s, d)
```

```python
import functools

import numpy as np
import jax
import jax.numpy as jnp
from jax import lax
from jax.experimental import pallas as pl
from jax.experimental.pallas import tpu as pltpu

F32 = jnp.float32
BF16 = jnp.bfloat16

RET_HEADS = 8
RET_ROT_BASE = 10000.0
SWA_HEAD_DIM = 64
SWA_KV_HEADS = 8
SWA_BLOCK = 128
EPS = 1e-6

RET_CHUNK = 256

V7X_VMEM_BYTES = 64 * 1024 * 1024
VMEM_LIMIT_BYTES = V7X_VMEM_BYTES - 2 * 1024 * 1024

NEG_BIG = -1e30


def _params(semantics):
    return pltpu.CompilerParams(dimension_semantics=semantics, vmem_limit_bytes=VMEM_LIMIT_BYTES)


def _sigmoid(y):
    return 0.5 * jnp.tanh(0.5 * y) + 0.5


def _mod_kernel(ct_ref, w_ref, b_ref, o_ref, *, batch):
    ct = ct_ref[...]
    ca = ct * _sigmoid(ct)
    w = w_ref[...]
    for b in range(batch):
        o_ref[b:b + 1, :] = jnp.sum(w * ca[:, b:b + 1], axis=0, keepdims=True) + b_ref[...]


def _modulation(c_t, ada_w, ada_b3, batch, tn):
    depth, d, n3 = ada_w.shape
    cols = c_t.shape[1]
    return pl.pallas_call(
        functools.partial(_mod_kernel, batch=batch),
        grid=(depth, n3 // tn),
        in_specs=[
            pl.BlockSpec((d, cols), lambda l, j: (0, 0)),
            pl.BlockSpec((None, d, tn), lambda l, j: (l, 0, j)),
            pl.BlockSpec((None, 1, tn), lambda l, j: (l, 0, j)),
        ],
        out_specs=pl.BlockSpec((None, batch, tn), lambda l, j: (l, 0, j)),
        out_shape=jax.ShapeDtypeStruct((depth, batch, n3), F32),
        compiler_params=_params(("parallel", "parallel")),
        name="adaln_modulation",
    )(c_t, ada_w, ada_b3)


def _modulated_norm(h, nw, scale, shift):
    ms = jnp.mean(h * h, axis=-1, keepdims=True)
    return h * lax.rsqrt(ms + EPS) * (nw * (1.0 + scale)) + shift


def _u_kernel(x_ref, nw_ref, sc_ref, sh_ref, u_ref):
    u_ref[...] = _modulated_norm(x_ref[...], nw_ref[...], sc_ref[...], sh_ref[...]).astype(BF16)


def _first_u(x2, nw, scale, shift, seq, tm):
    m, d = x2.shape
    per_batch = seq // tm
    return pl.pallas_call(
        _u_kernel,
        grid=(m // tm,),
        in_specs=[
            pl.BlockSpec((tm, d), lambda i: (i, 0)),
            pl.BlockSpec((1, d), lambda i: (0, 0)),
            pl.BlockSpec((None, 1, d), lambda i: (i // per_batch, 0, 0)),
            pl.BlockSpec((None, 1, d), lambda i: (i // per_batch, 0, 0)),
        ],
        out_specs=pl.BlockSpec((tm, d), lambda i: (i, 0)),
        out_shape=jax.ShapeDtypeStruct((m, d), BF16),
        compiler_params=_params(("parallel",)),
        name="first_modulated_norm",
    )(x2, nw, scale, shift)


def _inproj_rot_kernel(u_ref, w_ref, cos_ref, sin_ref, o_ref, wbf_ref, *, halves):
    @pl.when(pl.program_id(1) == 0)
    def _():
        wbf_ref[...] = w_ref[...].astype(BF16)

    hd = cos_ref.shape[-1]
    sub = u_ref.shape[0] // halves
    for part in range(halves):
        rows = slice(part * sub, (part + 1) * sub)
        y = jnp.dot(u_ref[rows, :], wbf_ref[...], preferred_element_type=F32)
        cs = cos_ref[rows, :]
        sn = sin_ref[rows, :]
        for h in range(y.shape[1] // hd):
            yh = y[:, h * hd:(h + 1) * hd]
            o_ref[rows, h * hd:(h + 1) * hd] = (yh * cs + pltpu.roll(yh, hd // 2, 1) * sn).astype(BF16)


def _inproj_gen_kernel(*refs, halves, rot_tiles, silu_tiles, sigmoid_tiles, n_cast):
    u_ref, w_ref, cos_ref, sin_ref = refs[:4]
    cast_in = refs[4:4 + n_cast]
    o_ref = refs[4 + n_cast]
    cast_out = refs[5 + n_cast:5 + 2 * n_cast]
    wbf_ref = refs[5 + 2 * n_cast]

    @pl.when(pl.program_id(1) == 0)
    def _():
        wbf_ref[...] = w_ref[...].astype(BF16)

    j = pl.program_id(0)
    is_rot = jnp.logical_and(j >= rot_tiles[0], j < rot_tiles[1])
    is_silu = jnp.logical_and(j >= silu_tiles[0], j < silu_tiles[1])
    is_sig = jnp.logical_and(j >= sigmoid_tiles[0], j < sigmoid_tiles[1])
    sub = u_ref.shape[0] // halves
    hd = cos_ref.shape[-1]

    def body(epilogue, rotate=False):
        for src, dst in zip(cast_in, cast_out):
            dst[...] = src[...].astype(BF16)
        for part in range(halves):
            rows = slice(part * sub, (part + 1) * sub)
            y = jnp.dot(u_ref[rows, :], wbf_ref[...], preferred_element_type=F32)
            if not rotate:
                o_ref[rows, :] = epilogue(y).astype(BF16)
                continue
            cs = cos_ref[rows, :]
            sn = sin_ref[rows, :]
            for h in range(y.shape[1] // hd):
                yh = y[:, h * hd:(h + 1) * hd]
                o_ref[rows, h * hd:(h + 1) * hd] = (yh * cs + pltpu.roll(yh, hd // 2, 1) * sn).astype(BF16)

    @pl.when(is_rot)
    def _():
        body(None, rotate=True)

    @pl.when(is_silu)
    def _():
        body(lambda y: y * _sigmoid(y))

    @pl.when(is_sig)
    def _():
        body(_sigmoid)

    @pl.when(jnp.logical_not(jnp.logical_or(is_rot, jnp.logical_or(is_silu, is_sig))))
    def _():
        body(lambda y: y)


def _inproj(kernel_fn, u, w_in, layer, col_map, n_tiles, tm, tn, name, extra_specs=(), extra_args=(),
            extra_out_specs=(), extra_out_shapes=()):
    m, d = u.shape
    return pl.pallas_call(
        kernel_fn,
        grid=(n_tiles, m // tm),
        in_specs=[
            pl.BlockSpec((tm, d), lambda j, i: (i, 0)),
            pl.BlockSpec((None, d, tn), lambda j, i: (layer, 0, col_map(j))),
        ] + list(extra_specs),
        out_specs=[pl.BlockSpec((tm, tn), lambda j, i: (i, j))] + list(extra_out_specs),
        out_shape=[jax.ShapeDtypeStruct((m, n_tiles * tn), BF16)] + list(extra_out_shapes),
        scratch_shapes=[pltpu.VMEM((d, tn), BF16)],
        compiler_params=_params(("arbitrary", "arbitrary")),
        name=name,
    )(u, w_in, *extra_args)


def _cast_slab_specs(weights, layer, n_steps, row_tiles):
    n_slabs = 1 << ((n_steps // len(weights)).bit_length() - 1)
    in_specs, out_specs, out_shapes = [], [], []
    for idx, w in enumerate(weights):
        rows, cols = w.shape[1:]
        slab = rows // n_slabs
        assert slab * n_slabs == rows and slab % 16 == 0

        def slab_of(j, i, idx=idx):
            return jnp.clip(j * row_tiles + i - idx * n_slabs, 0, n_slabs - 1)

        in_specs.append(pl.BlockSpec((None, slab, cols), lambda j, i, f=slab_of: (layer, f(j, i), 0)))
        out_specs.append(pl.BlockSpec((slab, cols), lambda j, i, f=slab_of: (f(j, i), 0)))
        out_shapes.append(jax.ShapeDtypeStruct((rows, cols), BF16))
    return in_specs, out_specs, out_shapes


def _retention_heads(qk_ref, v_ref, gnw_ref, o_ref, state_ref, head_list, *, heads, chunk, log_gammas):
    t = qk_ref.shape[0]
    dk = qk_ref.shape[1] // (2 * heads)
    dv = v_ref.shape[1] // heads
    row = lax.broadcasted_iota(jnp.int32, (chunk, chunk), 0)
    col = lax.broadcasted_iota(jnp.int32, (chunk, chunk), 1)
    causal = row >= col
    pos = lax.broadcasted_iota(jnp.int32, (chunk, 1), 0).astype(F32)

    for h in head_list:
        lg = log_gammas[h]
        q_scale = jnp.exp(lg * pos)
        k_scale = jnp.exp(-lg * pos)
        g_chunk = float(np.exp(lg * chunk))
        gnw = gnw_ref[:, h * dv:(h + 1) * dv]
        carry = state_ref[h]
        for c in range(t // chunk):
            rows = slice(c * chunk, (c + 1) * chunk)
            qa = (qk_ref[rows, h * dk:(h + 1) * dk].astype(F32) * q_scale).astype(BF16)
            kb = (qk_ref[rows, (heads + h) * dk:(heads + h + 1) * dk].astype(F32) * k_scale).astype(BF16)
            v = v_ref[rows, h * dv:(h + 1) * dv]
            s = lax.dot_general(qa, kb, (((1,), (1,)), ((), ())), preferred_element_type=F32)
            lhs = jnp.concatenate([jnp.where(causal, s, 0.0).astype(BF16), qa], axis=1)
            rhs = jnp.concatenate([v, carry.astype(BF16)], axis=0)
            o = jnp.dot(lhs, rhs, preferred_element_type=F32)
            kv = lax.dot_general(kb, v, (((0,), (0,)), ((), ())), preferred_element_type=F32)
            carry = (carry + kv) * g_chunk
            mu = jnp.mean(o, axis=-1, keepdims=True)
            dlt = o - mu
            var = jnp.mean(dlt * dlt, axis=-1, keepdims=True)
            o_ref[rows, h * dv:(h + 1) * dv] = (dlt * lax.rsqrt(var + EPS) * gnw).astype(BF16)
        state_ref[h] = carry


def _swa_body(sink_ref, q_ref, kp_ref, kc_ref, vp_ref, vc_ref, g_ref, o_ref, *, kv_heads, group, hd, blk,
              first):
    lanes = 2 * hd
    log2e = float(np.log2(np.e))
    lane = lax.broadcasted_iota(jnp.int32, (1, lanes), 1)
    lo_f = lane < hd
    scale = float(hd) ** -0.5 * log2e
    qmask = (jnp.where(lo_f, scale, 0.0).astype(BF16), jnp.where(lo_f, 0.0, scale).astype(BF16))
    key = lax.broadcasted_iota(jnp.int32, (blk, 2 * blk), 0)
    qry = jnp.bitwise_and(lax.broadcasted_iota(jnp.int32, (blk, 2 * blk), 1), blk - 1)
    upper = key > qry
    head_lo = lax.broadcasted_iota(jnp.int32, (1, 2 * blk), 1) < blk

    def dup(tile_bf16, parity):
        tile = tile_bf16.astype(F32)
        rolled = pltpu.roll(tile, hd, 1)
        keep_lo = lo_f if parity == 0 else jnp.logical_not(lo_f)
        return jnp.where(keep_lo, tile, rolled)

    def key_value_operands(k_tiles, v_tiles):
        ks = [dup(k_tiles[:, (h // 2) * lanes:(h // 2 + 1) * lanes], h % 2).astype(BF16)
              for h in range(kv_heads)]
        vts = [v_tiles[:, t * lanes:(t + 1) * lanes].astype(F32).T.astype(BF16)
               for t in range(kv_heads // 2)]
        return ks, vts

    nt = (((1,), (1,)), ((), ()))
    prev_ops = None if first else key_value_operands(kp_ref[...], vp_ref[...])
    for bi in range(q_ref.shape[0] // blk):
        rows = slice(bi * blk, (bi + 1) * blk)
        cur_ops = key_value_operands(kc_ref[rows, :], vc_ref[rows, :])
        no_prev = prev_ops is None
        for h in range(kv_heads):
            hrows = slice((h % 2) * hd, (h % 2 + 1) * hd)
            k_cur = cur_ops[0][h]
            vt_cur = cur_ops[1][h // 2]
            for gp in range(group // 2):
                qt = (h * group) // 2 + gp
                qcol = slice(qt * lanes, (qt + 1) * lanes)
                qtile = q_ref[rows, qcol]
                qq = jnp.concatenate([qtile * qmask[0], qtile * qmask[1]], axis=0)
                s_cur = lax.dot_general(k_cur, qq, nt, preferred_element_type=F32)
                if no_prev:
                    s = jnp.where(upper, NEG_BIG, s_cur)
                else:
                    s = jnp.where(upper, lax.dot_general(prev_ops[0][h], qq, nt, preferred_element_type=F32),
                                  s_cur)
                hq = h * group + 2 * gp
                sink = jnp.where(head_lo, sink_ref[hq], sink_ref[hq + 1]) * log2e
                mx = jnp.maximum(jnp.max(s, axis=0, keepdims=True), sink)
                p = jnp.exp2(s - mx)
                inv = 1.0 / (jnp.sum(p, axis=0, keepdims=True) + jnp.exp2(sink - mx))
                if no_prev:
                    ot = jnp.dot(vt_cur, p.astype(BF16), preferred_element_type=F32)
                else:
                    ot = (jnp.dot(prev_ops[1][h // 2], jnp.where(upper, p, 0.0).astype(BF16),
                                  preferred_element_type=F32)
                          + jnp.dot(vt_cur, jnp.where(upper, 0.0, p).astype(BF16), preferred_element_type=F32))
                ot = ot[hrows] * inv
                pair_t = jnp.concatenate([ot[:, :blk], ot[:, blk:]], axis=0)
                o_ref[rows, qcol] = (pair_t.T * g_ref[rows, qcol].astype(F32)).astype(BF16)
        prev_ops = cur_ops


def _swa_kernel(*refs, **kw):
    is_first = pl.program_id(1) == 0

    @pl.when(is_first)
    def _():
        _swa_body(*refs, first=True, **kw)

    @pl.when(jnp.logical_not(is_first))
    def _():
        _swa_body(*refs, first=False, **kw)


def _swa(proj, sinks_l, seq, q_w, kv_w, q_blk, k_blk, v_blk, g_blk, blocks_per_step):
    m = proj.shape[0]
    t = blocks_per_step * SWA_BLOCK
    steps = seq // t
    kv_heads = SWA_KV_HEADS
    group = q_w // SWA_HEAD_DIM // kv_heads
    cur = lambda b, n: b * steps + n
    prev = lambda b, n: (b * steps + n) * blocks_per_step - jnp.minimum(n, 1)
    return pl.pallas_call(
        functools.partial(_swa_kernel, kv_heads=kv_heads, group=group, hd=SWA_HEAD_DIM, blk=SWA_BLOCK),
        grid=(m // seq, steps),
        in_specs=[
            pl.BlockSpec(memory_space=pltpu.SMEM),
            pl.BlockSpec((t, q_w), lambda b, n: (cur(b, n), q_blk)),
            pl.BlockSpec((SWA_BLOCK, kv_w), lambda b, n: (prev(b, n), k_blk)),
            pl.BlockSpec((t, kv_w), lambda b, n: (cur(b, n), k_blk)),
            pl.BlockSpec((SWA_BLOCK, kv_w), lambda b, n: (prev(b, n), v_blk)),
            pl.BlockSpec((t, kv_w), lambda b, n: (cur(b, n), v_blk)),
            pl.BlockSpec((t, q_w), lambda b, n: (cur(b, n), g_blk)),
        ],
        out_specs=pl.BlockSpec((t, q_w), lambda b, n: (cur(b, n), 0)),
        out_shape=jax.ShapeDtypeStruct((m, q_w), BF16),
        compiler_params=_params(("parallel", "arbitrary")),
        name="swa",
    )(sinks_l, proj, proj, proj, proj, proj, proj)


def _outproj_kernel(*refs, last, ret_kw, tiles_per_seq):
    (qk0_ref, rv0_ref, qk_ref, rv_ref, gnw_ref, rg_ref, a_ref, sg_ref, h_ref, gate_ref,
     wr_ref, ws_ref, wo_ref) = refs[:13]
    if last:
        fw_ref, out_ref, rbuf_ref, state_ref = refs[13:]
    else:
        nw_ref, sc_ref, sh_ref, hn_ref, u_ref, rbuf_ref, state_ref = refs[13:]
    d = h_ref.shape[1]
    i = pl.program_id(0)
    heads = ret_kw["heads"]
    slot = lax.rem(i, 2)

    @pl.when(i == 0)
    def _():
        state_ref[...] = jnp.zeros_like(state_ref)
        _retention_heads(qk0_ref, rv0_ref, gnw_ref, rbuf_ref.at[0], state_ref, range(heads), **ret_kw)

    @pl.when(lax.rem(i + 1, tiles_per_seq) == 0)
    def _():
        state_ref[...] = jnp.zeros_like(state_ref)

    nxt_buf = rbuf_ref.at[1 - slot]
    y1 = jnp.dot(rbuf_ref[slot] * rg_ref[...], wr_ref[...], preferred_element_type=F32)
    _retention_heads(qk_ref, rv_ref, gnw_ref, nxt_buf, state_ref, range(0, heads // 2), **ret_kw)
    y2 = jnp.dot(a_ref[...], ws_ref[...], preferred_element_type=F32)
    _retention_heads(qk_ref, rv_ref, gnw_ref, nxt_buf, state_ref, range(heads // 2, heads), **ret_kw)
    merged = sg_ref[:, :d].astype(F32) * y1 + sg_ref[:, d:].astype(F32) * y2
    z = jnp.dot(merged.astype(BF16), wo_ref[...], preferred_element_type=F32)
    hn = h_ref[...] + gate_ref[...] * z
    if last:
        ms = jnp.mean(hn * hn, axis=-1, keepdims=True)
        out_ref[...] = hn * lax.rsqrt(ms + EPS) * fw_ref[...]
    else:
        hn_ref[...] = hn
        u_ref[...] = _modulated_norm(hn, nw_ref[...], sc_ref[...], sh_ref[...]).astype(BF16)


def _outproj(rqk, gn_w3, layer, a, proj, rv_blk, rg_blk, sig_blk, h, gate, wr, ws, wo, seq, tm,
             nxt=None, final_w=None, qk_blk=0):
    m, d = h.shape
    v_w = gn_w3.shape[-1]
    qk_w = v_w
    n = m // tm
    per_batch = seq // tm
    assert tm == RET_CHUNK
    last = nxt is None
    row = lambda i: (i, 0)
    nxt_row = lambda i: jnp.minimum(i + 1, n - 1)
    vec = pl.BlockSpec((None, 1, d), lambda i: (i // per_batch, 0, 0))
    once = lambda shape, col: pl.BlockSpec(shape, lambda i: (0, col), pipeline_mode=pl.Buffered(1))
    in_specs = [
        once((tm, qk_w), qk_blk), once((tm, v_w), rv_blk),
        pl.BlockSpec((tm, qk_w), lambda i: (nxt_row(i), qk_blk)),
        pl.BlockSpec((tm, v_w), lambda i: (nxt_row(i), rv_blk)),
        pl.BlockSpec((None, 1, v_w), lambda i: (layer, 0, 0)),
        pl.BlockSpec((tm, v_w), lambda i: (i, rg_blk)),
        pl.BlockSpec((tm, d), row), pl.BlockSpec((tm, 2 * d), lambda i: (i, sig_blk)),
        pl.BlockSpec((tm, d), row), vec, once((v_w, d), 0), once((d, d), 0), once((d, d), 0),
    ]
    args = [rqk, proj, rqk, proj, gn_w3, proj, a, proj, h, gate, wr, ws, wo]
    log_gammas = tuple(float(np.log1p(-np.exp2(-5.0 - hh))) for hh in range(RET_HEADS))
    ret_kw = dict(heads=RET_HEADS, chunk=RET_CHUNK, log_gammas=log_gammas)
    scratch = [pltpu.VMEM((2, tm, v_w), BF16),
               pltpu.VMEM((RET_HEADS, qk_w // (2 * RET_HEADS), v_w // RET_HEADS), F32)]
    if last:
        in_specs.append(pl.BlockSpec((1, d), lambda i: (0, 0)))
        args.append(final_w)
        out_specs = pl.BlockSpec((tm, d), row)
        out_shape = jax.ShapeDtypeStruct((m, d), F32)
    else:
        nw, sc, sh = nxt
        in_specs += [pl.BlockSpec((1, d), lambda i: (0, 0)), vec, vec]
        args += [nw, sc, sh]
        out_specs = (pl.BlockSpec((tm, d), row), pl.BlockSpec((tm, d), row))
        out_shape = (jax.ShapeDtypeStruct((m, d), F32), jax.ShapeDtypeStruct((m, d), BF16))
    return pl.pallas_call(
        functools.partial(_outproj_kernel, last=last, ret_kw=ret_kw, tiles_per_seq=per_batch),
        grid=(n,),
        in_specs=in_specs,
        out_specs=out_specs,
        out_shape=out_shape,
        scratch_shapes=scratch,
        compiler_params=_params(("arbitrary",)),
        name="outproj_last" if last else "outproj",
    )(*args)


def kernel(x, c, norm_w, ada_w, ada_b, w_in, ret_gn_w, attn_sinks, w_ret_o, w_swa_o, w_out, final_norm_w):
    b, s, d = x.shape
    depth = norm_w.shape[0]
    m = b * s
    ret_v_w = ret_gn_w.shape[1]
    ret_qk_w = ret_v_w // 2
    swa_q_w = d
    swa_kv_w = SWA_KV_HEADS * SWA_HEAD_DIM
    tn = 1024
    tm = 2048
    assert w_in.shape[2] == 2 * ret_qk_w + 2 * ret_v_w + 2 * swa_q_w + 2 * swa_kv_w + 2 * d
    assert ret_qk_w == tn and 2 * swa_kv_w == tn and ret_v_w == 2 * tn and d == 2 * tn and s % tm == 0

    c_t = jnp.pad(c.T, ((0, 0), (0, 8 - b)))
    mod = _modulation(c_t, ada_w, ada_b.reshape(depth, 1, 3 * d), b, tn=1536)
    shift = mod[:, :, :d].reshape(depth, b, 1, d)
    scale = mod[:, :, d:2 * d].reshape(depth, b, 1, d)
    gate = mod[:, :, 2 * d:].reshape(depth, b, 1, d)

    half = ret_qk_w // RET_HEADS // 2
    inv = (1.0 / (RET_ROT_BASE ** np.linspace(0.0, 1.0, half, dtype=np.float32))).astype(np.float32)
    ang = np.arange(s, dtype=np.float32)[:, None] * inv[None, :]
    cos2 = np.concatenate([np.cos(ang), np.cos(ang)], axis=-1)
    sin2 = np.concatenate([-np.sin(ang), np.sin(ang)], axis=-1)
    k_scale = np.float32(float(2 * half) ** -0.5)
    cos_tab = jnp.asarray(np.stack([cos2, cos2 * k_scale]), dtype=F32)
    sin_tab = jnp.asarray(np.stack([sin2, sin2 * k_scale]), dtype=F32)

    gn_w3 = ret_gn_w.reshape(depth, 1, ret_v_w)
    nw2 = norm_w.reshape(depth, 1, d)

    h = x.reshape(m, d)
    u = _first_u(h, nw2[0], scale[0], shift[0], s, tm=1024)

    gen_map = lambda j: jnp.where(j < 2, j + 2, jnp.where(j < 4, j + 4, jnp.where(j < 6, j,
                                  jnp.where(j < 12, j + 3, jnp.where(j < 14, j - 12, 8)))))
    gen_tiles = 15
    off_rv, off_sq, off_rg, off_sg, off_sig, off_rqk, off_sk = (i * tn for i in (0, 2, 4, 6, 8, 12, 14))
    off_sv = off_sk + swa_kv_w
    out_weights = (w_ret_o, w_swa_o, w_out)
    gen_kernel = functools.partial(_inproj_gen_kernel, halves=2, rot_tiles=(12, 14), silu_tiles=(4, 8),
                                   sigmoid_tiles=(8, 12), n_cast=len(out_weights))
    per_batch = s // tm
    hd2 = cos_tab.shape[-1]
    tab_spec = pl.BlockSpec((None, tm, hd2), lambda j, i: (jnp.clip(j - 12, 0, 1), i % per_batch, 0))

    for l in range(depth):
        cast_in, cast_out, cast_shapes = _cast_slab_specs(out_weights, l, gen_tiles * (m // tm), m // tm)
        proj, wr_bf, ws_bf, wo_bf = _inproj(gen_kernel, u, w_in, l, gen_map, gen_tiles, tm, tn, "inproj_gen",
                                            extra_specs=[tab_spec, tab_spec] + cast_in,
                                            extra_args=(cos_tab, sin_tab) + out_weights,
                                            extra_out_specs=cast_out, extra_out_shapes=cast_shapes)
        rqk = proj
        a = _swa(proj, attn_sinks[l], s, swa_q_w, swa_kv_w, q_blk=off_sq // swa_q_w, k_blk=off_sk // swa_kv_w,
                 v_blk=off_sv // swa_kv_w, g_blk=off_sg // swa_q_w, blocks_per_step=4)
        out_args = (rqk, gn_w3, l, a, proj, off_rv // ret_v_w, off_rg // ret_v_w, off_sig // (2 * d),
                    h, gate[l], wr_bf, ws_bf, wo_bf, s)
        if l + 1 < depth:
            h, u = _outproj(*out_args, tm=256, qk_blk=off_rqk // ret_v_w,
                            nxt=(nw2[l + 1], scale[l + 1], shift[l + 1]))
        else:
            h = _outproj(*out_args, tm=256, qk_blk=off_rqk // ret_v_w, final_w=final_norm_w.reshape(1, d))
    return h.reshape(b, s, d)
```
